```python
import math
import jax
import jax.numpy as jnp
from jax import lax
import numpy as np

D_MODEL = 4096
BATCH = 2
SEQ = 8192
DEPTH = 1

GRID_W = 64
CTX_LEN = 256
MIX_W = D_MODEL
ATT_W = MIX_W // 2
CONV_W = MIX_W - ATT_W
HEAD_DIM = 64
V_HEAD_DIM = 2 * HEAD_DIM
N_HEADS = ATT_W // V_HEAD_DIM
QK_W = N_HEADS * 2 * HEAD_DIM
V_W = N_HEADS * V_HEAD_DIM
PROJ_W = 2 * QK_W + V_W + 2 * CONV_W
CONV_K = 31
FFN_CONV_K = 3
D_FF = ((8 * D_MODEL // 3) + 255) // 256 * 256
Q_BLOCK = 128
ROPE_BASE = 10000.0
N_MOD = 6

kernel_name = "hybrid_diffattn_conformer_convffn_dit_block"


def rms_norm(x, g, eps=1e-6):
    xf = x.astype(jnp.float32)
    y = xf * lax.rsqrt(jnp.mean(xf * xf, axis=-1, keepdims=True) + eps)
    return (y * g.astype(jnp.float32)).astype(x.dtype)


def layer_norm(x, g, b, eps=1e-5):
    xf = x.astype(jnp.float32)
    mu = jnp.mean(xf, axis=-1, keepdims=True)
    var = jnp.mean(jnp.square(xf - mu), axis=-1, keepdims=True)
    y = (xf - mu) * lax.rsqrt(var + eps)
    return (y * g.astype(jnp.float32) + b.astype(jnp.float32)).astype(x.dtype)


def modulate(h, shift, scale):
    return h * (1 + scale) + shift


def dwconv(x, w, b):
    k = w.shape[0]
    pad = (k - 1) // 2
    y = lax.conv_general_dilated(
        x, w[:, None, :].astype(x.dtype), window_strides=(1,),
        padding=[(pad, pad)], dimension_numbers=('NWC', 'WIO', 'NWC'),
        feature_group_count=x.shape[-1])
    return y + b.astype(x.dtype)


def axial_rope_tables(n_tokens):
    rows = n_tokens // GRID_W
    row = jnp.broadcast_to(jnp.arange(rows)[:, None], (rows, GRID_W)).reshape(-1)
    col = jnp.broadcast_to(jnp.arange(GRID_W)[None, :], (rows, GRID_W)).reshape(-1)
    n_pairs_axis = HEAD_DIM // 4
    freqs = ROPE_BASE ** (-jnp.arange(n_pairs_axis, dtype=jnp.float32) / n_pairs_axis)
    ang = jnp.concatenate([row.astype(jnp.float32)[:, None] * freqs,
                           col.astype(jnp.float32)[:, None] * freqs], axis=-1)
    return jnp.cos(ang), jnp.sin(ang)


def apply_rope(x, cos, sin):
    xp = x.reshape(x.shape[:-1] + (HEAD_DIM // 2, 2))
    xe, xo = xp[..., 0], xp[..., 1]
    c = cos[None, :, None, None, :].astype(x.dtype)
    s = sin[None, :, None, None, :].astype(x.dtype)
    out = jnp.stack([xe * c - xo * s, xe * s + xo * c], axis=-1)
    return out.reshape(x.shape)


def split_proj(p):
    return jnp.split(p, [QK_W, 2 * QK_W, 2 * QK_W + V_W, 2 * QK_W + V_W + CONV_W], axis=-1)


def qk_heads(t, g):
    b, n = t.shape[:2]
    return rms_norm(t.reshape(b, n, N_HEADS, 2, HEAD_DIM), g)


def diff_lambda(lq1, lk1, lq2, lk2, lam_init):
    f = jnp.float32
    return (jnp.exp(jnp.sum(lq1.astype(f) * lk1.astype(f)))
            - jnp.exp(jnp.sum(lq2.astype(f) * lk2.astype(f))) + lam_init)


def diff_attn_blocks(q, k, v, lam):
    b, n = q.shape[:2]
    nb = n // Q_BLOCK
    qb = (q * (HEAD_DIM ** -0.5)).reshape(b, nb, Q_BLOCK, N_HEADS, 2, HEAD_DIM)
    qb = qb.transpose(1, 0, 3, 4, 2, 5)
    kt = k.transpose(0, 2, 3, 1, 4)
    vt = v.transpose(0, 2, 1, 3)

    def block(qblk):
        s = jnp.einsum('bhiqd,bhikd->bhiqk', qblk, kt).astype(jnp.float32)
        p = jax.nn.softmax(s, axis=-1)
        a = p[:, :, 0] - lam * p[:, :, 1]
        return jnp.einsum('bhqk,bhkd->bhqd', a.astype(vt.dtype), vt)

    o = lax.map(block, qb)
    return o.transpose(1, 0, 3, 2, 4).reshape(b, n, N_HEADS, V_HEAD_DIM)


def diff_attn_dense(q, k, v, lam):
    s = jnp.einsum('bqhid,bkhid->bhiqk', q * (HEAD_DIM ** -0.5), k).astype(jnp.float32)
    p = jax.nn.softmax(s, axis=-1)
    a = p[:, :, 0] - lam * p[:, :, 1]
    return jnp.einsum('bhqk,bkhd->bqhd', a.astype(v.dtype), v)


def diff_head_out(o, subln_g, lam_init):
    b, n = o.shape[:2]
    return (rms_norm(o, subln_g) * (1 - lam_init)).reshape(b, n, V_W)


def conformer_conv(ca, cg, dw_w, dw_b, ln_g, ln_b):
    u = ca * jax.nn.sigmoid(cg)
    u = dwconv(u, dw_w, dw_b)
    u = layer_norm(u, ln_g, ln_b)
    return jax.nn.silu(u)


def conv_ffn(h, w_up, dw_w, dw_b, w_down):
    up = dwconv(h @ w_up, dw_w, dw_b)
    u, g = jnp.split(up, 2, axis=-1)
    return (jax.nn.silu(g) * u) @ w_down


def setup_inputs(seed: int = 0) -> dict:
    key = jax.random.key(seed)
    ks = jax.random.split(key, 32)

    def nrm(k, shape, scale):
        return jax.random.normal(k, shape, jnp.float32) * scale

    L, D = DEPTH, D_MODEL
    return {
        "x": nrm(ks[0], (BATCH, SEQ, D), 1.0),
        "c": nrm(ks[1], (BATCH, D), 1.0),
        "ctx": nrm(ks[2], (BATCH, CTX_LEN, D), 1.0),
        "c_ctx": nrm(ks[3], (D,), 1.0),
        "w_ada": nrm(ks[4], (L, D, N_MOD * D), D ** -0.5),
        "b_ada": nrm(ks[5], (L, N_MOD * D), 0.02),
        "norm1_g": 1.0 + nrm(ks[6], (L, D), 0.05),
        "norm2_g": 1.0 + nrm(ks[7], (L, D), 0.05),
        "w_in": nrm(ks[8], (L, D, PROJ_W), D ** -0.5),
        "q_norm_g": 1.0 + nrm(ks[9], (L, HEAD_DIM), 0.05),
        "k_norm_g": 1.0 + nrm(ks[10], (L, HEAD_DIM), 0.05),
        "lambda_q1": nrm(ks[11], (L, HEAD_DIM), 0.1),
        "lambda_k1": nrm(ks[12], (L, HEAD_DIM), 0.1),
        "lambda_q2": nrm(ks[13], (L, HEAD_DIM), 0.1),
        "lambda_k2": nrm(ks[14], (L, HEAD_DIM), 0.1),
        "subln_g": 1.0 + nrm(ks[15], (L, V_HEAD_DIM), 0.05),
        "conv_dw_w": nrm(ks[16], (L, CONV_K, CONV_W), CONV_K ** -0.5),
        "conv_dw_b": nrm(ks[17], (L, CONV_W), 0.02),
        "conv_ln_g": 1.0 + nrm(ks[18], (L, CONV_W), 0.05),
        "conv_ln_b": nrm(ks[19], (L, CONV_W), 0.02),
        "w_out": nrm(ks[20], (L, MIX_W, D), MIX_W ** -0.5),
        "w_up": nrm(ks[21], (L, D, 2 * D_FF), D ** -0.5),
        "ffn_dw_w": nrm(ks[22], (L, FFN_CONV_K, 2 * D_FF), FFN_CONV_K ** -0.5),
        "ffn_dw_b": nrm(ks[23], (L, 2 * D_FF), 0.02),
        "w_down": nrm(ks[24], (L, D_FF, D), D_FF ** -0.5),
    }


def reference(x, c, ctx, c_ctx, w_ada, b_ada, norm1_g, norm2_g, w_in, q_norm_g, k_norm_g,
              lambda_q1, lambda_k1, lambda_q2, lambda_k2, subln_g, conv_dw_w, conv_dw_b,
              conv_ln_g, conv_ln_b, w_out, w_up, ffn_dw_w, ffn_dw_b, w_down):
    b, n, d = x.shape
    n_ctx = ctx.shape[1]
    cos, sin = axial_rope_tables(n)
    for i in range(DEPTH):
        update_ctx = i < DEPTH - 1
        lam_init = 0.8 - 0.6 * math.exp(-0.3 * i)
        lam = diff_lambda(lambda_q1[i], lambda_k1[i], lambda_q2[i], lambda_k2[i], lam_init)

        mod_x = (jax.nn.silu(c) @ w_ada[i] + b_ada[i]).reshape(b, N_MOD, d)
        mod_x = mod_x.transpose(1, 0, 2)[:, :, None, :]
        mod_c = (jax.nn.silu(c_ctx) @ w_ada[i] + b_ada[i]).reshape(N_MOD, 1, 1, d)

        h_x = modulate(rms_norm(x, norm1_g[i]), mod_x[0], mod_x[1])
        h_c = modulate(rms_norm(ctx, norm1_g[i]), mod_c[0], mod_c[1])

        q_x, k_x, v_x, ca_x, cg_x = split_proj(h_x @ w_in[i])
        q_x = apply_rope(qk_heads(q_x, q_norm_g[i]), cos, sin)
        k_x = apply_rope(qk_heads(k_x, k_norm_g[i]), cos, sin)
        v_x = v_x.reshape(b, n, N_HEADS, V_HEAD_DIM)

        if update_ctx:
            q_c, k_c, v_c, ca_c, cg_c = split_proj(h_c @ w_in[i])
        else:
            k_c, v_c = jnp.split(h_c @ w_in[i][:, QK_W:2 * QK_W + V_W], [QK_W], axis=-1)
        k_c = qk_heads(k_c, k_norm_g[i])
        v_c = v_c.reshape(b, n_ctx, N_HEADS, V_HEAD_DIM)

        o_x = diff_attn_blocks(q_x, jnp.concatenate([k_c, k_x], axis=1),
                               jnp.concatenate([v_c, v_x], axis=1), lam)
        att_x = diff_head_out(o_x, subln_g[i], lam_init)
        conv_x = conformer_conv(ca_x, cg_x, conv_dw_w[i], conv_dw_b[i], conv_ln_g[i], conv_ln_b[i])
        mix_x = jnp.concatenate([att_x, conv_x], axis=-1) @ w_out[i]

        if update_ctx:
            q_c = qk_heads(q_c, q_norm_g[i])
            att_c = diff_head_out(diff_attn_dense(q_c, k_c, v_c, lam), subln_g[i], lam_init)
            conv_c = conformer_conv(ca_c, cg_c, conv_dw_w[i], conv_dw_b[i], conv_ln_g[i], conv_ln_b[i])
            ctx = ctx + mod_c[2] * (jnp.concatenate([att_c, conv_c], axis=-1) @ w_out[i])
            h2_c = modulate(rms_norm(ctx, norm2_g[i]), mod_c[3], mod_c[4])
            ctx = ctx + mod_c[5] * conv_ffn(h2_c, w_up[i], ffn_dw_w[i], ffn_dw_b[i], w_down[i])

        x = x + mod_x[2] * mix_x

        h2_x = modulate(rms_norm(x, norm2_g[i]), mod_x[3], mod_x[4])
        x = x + mod_x[5] * conv_ffn(h2_x, w_up[i], ffn_dw_w[i], ffn_dw_b[i], w_down[i])
    return x
```

```python
import functools
import math

import jax
import jax.numpy as jnp
from jax import lax
from jax.experimental import pallas as pl
from jax.experimental.pallas import tpu as pltpu

F32 = jnp.float32
BF16 = jnp.bfloat16

HEAD_DIM = 64
V_HEAD_DIM = 2 * HEAD_DIM
GRID_W = 64
ROPE_BASE = 10000.0
N_MOD = 6
LANES = 128
HALO = 16
VMEM_LIMIT = 56 * 1024 * 1024


def _params(*sem):
    return pltpu.CompilerParams(dimension_semantics=sem, vmem_limit_bytes=VMEM_LIMIT)


def _pick(n, prefs):
    for p in prefs:
        if n % p == 0:
            return p
    return n


def _ada_kernel(c_ref, w_ref, b_ref, o_ref):
    c = c_ref[...]
    s = (c * jax.nn.sigmoid(c)).astype(BF16)
    o_ref[...] = jnp.dot(s, w_ref[...].astype(BF16), preferred_element_type=F32) + b_ref[...]


def _ada(cc, w_ada, b_ada):
    d, n = w_ada.shape
    tn = _pick(n, (512, 256, 128))
    return pl.pallas_call(
        _ada_kernel,
        grid=(n // tn,),
        in_specs=[pl.BlockSpec((8, d), lambda j: (0, 0)),
                  pl.BlockSpec((d, tn), lambda j: (0, j)),
                  pl.BlockSpec((1, tn), lambda j: (0, j))],
        out_specs=pl.BlockSpec((8, tn), lambda j: (0, j)),
        out_shape=jax.ShapeDtypeStruct((8, n), F32),
        compiler_params=_params("arbitrary"),
        name="ada",
    )(cc, w_ada, b_ada.reshape(1, n))


def _norm_mod_kernel(x_ref, g_ref, shift_ref, scale_ref, o_ref):
    x = x_ref[0]
    ms = jnp.mean(x * x, axis=-1, keepdims=True)
    y = x * lax.rsqrt(ms + 1e-6) * g_ref[...]
    o_ref[0] = (y * (1.0 + scale_ref[0]) + shift_ref[0]).astype(o_ref.dtype)


def _norm_mod(x, g, shift, scale):
    b, t, d = x.shape
    tm = _pick(t, (512, 256, 128))
    return pl.pallas_call(
        _norm_mod_kernel,
        grid=(b, t // tm),
        in_specs=[pl.BlockSpec((1, tm, d), lambda bi, i: (bi, i, 0)),
                  pl.BlockSpec((1, d), lambda bi, i: (0, 0)),
                  pl.BlockSpec((1, 1, d), lambda bi, i: (bi, 0, 0)),
                  pl.BlockSpec((1, 1, d), lambda bi, i: (bi, 0, 0))],
        out_specs=pl.BlockSpec((1, tm, d), lambda bi, i: (bi, i, 0)),
        out_shape=jax.ShapeDtypeStruct((b, t, d), BF16),
        compiler_params=_params("arbitrary", "arbitrary"),
        name="norm_mod",
    )(x, g.reshape(1, d), shift, scale)


def _mm_kernel(a_ref, b_ref, o_ref):
    o_ref[...] = jnp.dot(a_ref[...], b_ref[...], preferred_element_type=F32).astype(o_ref.dtype)


def _matmul(a, w, out_dtype, col0=0, ncols=None):
    m, k = a.shape
    n = w.shape[1] if ncols is None else ncols
    tm = _pick(m, (1024, 512, 256, 128))
    tn = _pick(math.gcd(n, col0) if col0 else n, (1024, 512, 256, 128))
    c0 = col0 // tn
    return pl.pallas_call(
        _mm_kernel,
        grid=(m // tm, n // tn),
        in_specs=[pl.BlockSpec((tm, k), lambda i, j: (i, 0)),
                  pl.BlockSpec((k, tn), lambda i, j: (0, c0 + j))],
        out_specs=pl.BlockSpec((tm, tn), lambda i, j: (i, j)),
        out_shape=jax.ShapeDtypeStruct((m, n), out_dtype),
        compiler_params=_params("arbitrary", "arbitrary"),
        name="matmul",
    )(a, w)


def _subhead_rms(x, g, ones_bd):
    sq = x * x
    hi = sq.astype(BF16)
    lo = (sq - hi.astype(F32)).astype(BF16)
    ss = (jnp.dot(hi, ones_bd, preferred_element_type=F32)
          + jnp.dot(lo, ones_bd, preferred_element_type=F32))
    return x * lax.rsqrt(ss * (1.0 / HEAD_DIM) + 1e-6) * g


def _rope(x, cos, sin_signed, even):
    w = x.shape[-1]
    nxt = pltpu.roll(x, w - 1, 1)
    prv = pltpu.roll(x, 1, 1)
    return x * cos + jnp.where(even, nxt, prv) * sin_signed


def _qkv_post_kernel(q_ref, k_ref, v_ref, cos_ref, sin_ref, gq_ref, gk_ref, ones_ref,
                     qo_ref, ko_ref, vo_ref, *, nh):
    ones_bd = ones_ref[...]
    cos = cos_ref[...]
    sin = sin_ref[...]
    lane = lax.broadcasted_iota(jnp.int32, cos.shape, 1)
    even = (lane % 2) == 0
    low = (lane % V_HEAD_DIM) < HEAD_DIM
    for h in range(nh):
        sl = slice(h * V_HEAD_DIM, (h + 1) * V_HEAD_DIM)
        q = _subhead_rms(q_ref[:, sl], gq_ref[...], ones_bd)
        q = _rope(q, cos, sin, even) * (HEAD_DIM ** -0.5)
        qo_ref[0, h, 0] = jnp.where(low, q, 0.0).astype(BF16)
        qo_ref[0, h, 1] = jnp.where(low, 0.0, q).astype(BF16)
        k = _subhead_rms(k_ref[:, sl], gk_ref[...], ones_bd)
        ko_ref[0, h] = _rope(k, cos, sin, even).astype(BF16)
        vo_ref[0, h] = v_ref[:, sl].astype(BF16)


def _kv_post_kernel(k_ref, v_ref, gk_ref, ones_ref, ko_ref, vo_ref, *, nh):
    ones_bd = ones_ref[...]
    for h in range(nh):
        sl = slice(h * V_HEAD_DIM, (h + 1) * V_HEAD_DIM)
        ko_ref[0, h] = _subhead_rms(k_ref[:, sl], gk_ref[...], ones_bd).astype(BF16)
        vo_ref[0, h] = v_ref[:, sl].astype(BF16)


def _ones_blockdiag():
    r = jnp.arange(LANES) // HEAD_DIM
    return (r[:, None] == r[None, :]).astype(BF16)


def _rope_tables(t):
    rows = t // GRID_W
    row = jnp.broadcast_to(jnp.arange(rows)[:, None], (rows, GRID_W)).reshape(-1)
    col = jnp.broadcast_to(jnp.arange(GRID_W)[None, :], (rows, GRID_W)).reshape(-1)
    n_pairs_axis = HEAD_DIM // 4
    freqs = ROPE_BASE ** (-jnp.arange(n_pairs_axis, dtype=F32) / n_pairs_axis)
    ang = jnp.concatenate([row.astype(F32)[:, None] * freqs,
                           col.astype(F32)[:, None] * freqs], axis=-1)
    cos = jnp.repeat(jnp.cos(ang), 2, axis=-1)
    sin = jnp.repeat(jnp.sin(ang), 2, axis=-1)
    sign = jnp.where(jnp.arange(HEAD_DIM) % 2 == 0, -1.0, 1.0).astype(F32)
    reps = LANES // HEAD_DIM
    return jnp.tile(cos, (1, reps)), jnp.tile(sin * sign, (1, reps))


def _qkv_post(p, b, t, n_heads, gq, gk):
    qk_w = n_heads * V_HEAD_DIM
    nh = _pick(n_heads, (4, 2, 1))
    wblk = nh * V_HEAD_DIM
    tm = _pick(t, (512, 256, 128))
    nt = t // tm
    nq = qk_w // wblk
    cos, sin = _rope_tables(t)
    g2 = lambda g: jnp.tile(g, LANES // HEAD_DIM).reshape(1, LANES)
    row = lambda bi, i, j: bi * nt + i
    return pl.pallas_call(
        functools.partial(_qkv_post_kernel, nh=nh),
        grid=(b, nt, nq),
        in_specs=[pl.BlockSpec((tm, wblk), lambda bi, i, j: (row(bi, i, j), j)),
                  pl.BlockSpec((tm, wblk), lambda bi, i, j: (row(bi, i, j), nq + j)),
                  pl.BlockSpec((tm, wblk), lambda bi, i, j: (row(bi, i, j), 2 * nq + j)),
                  pl.BlockSpec((tm, LANES), lambda bi, i, j: (i, 0)),
                  pl.BlockSpec((tm, LANES), lambda bi, i, j: (i, 0)),
                  pl.BlockSpec((1, LANES), lambda bi, i, j: (0, 0)),
                  pl.BlockSpec((1, LANES), lambda bi, i, j: (0, 0)),
                  pl.BlockSpec((LANES, LANES), lambda bi, i, j: (0, 0))],
        out_specs=[pl.BlockSpec((1, nh, 2, tm, LANES), lambda bi, i, j: (bi, j, 0, i, 0)),
                   pl.BlockSpec((1, nh, tm, LANES), lambda bi, i, j: (bi, j, i, 0)),
                   pl.BlockSpec((1, nh, tm, LANES), lambda bi, i, j: (bi, j, i, 0))],
        out_shape=[jax.ShapeDtypeStruct((b, n_heads, 2, t, LANES), BF16),
                   jax.ShapeDtypeStruct((b, n_heads, t, LANES), BF16),
                   jax.ShapeDtypeStruct((b, n_heads, t, LANES), BF16)],
        compiler_params=_params("arbitrary", "arbitrary", "arbitrary"),
        name="qkv_post",
    )(p, p, p, cos, sin, g2(gq), g2(gk), _ones_blockdiag())


def _kv_post(pc, b, t, n_heads, gk):
    qk_w = n_heads * V_HEAD_DIM
    nh = _pick(n_heads, (4, 2, 1))
    wblk = nh * V_HEAD_DIM
    tm = _pick(t, (256, 128))
    nt = t // tm
    nq = qk_w // wblk
    g2 = jnp.tile(gk, LANES // HEAD_DIM).reshape(1, LANES)
    return pl.pallas_call(
        functools.partial(_kv_post_kernel, nh=nh),
        grid=(b, nt, nq),
        in_specs=[pl.BlockSpec((tm, wblk), lambda bi, i, j: (bi * nt + i, j)),
                  pl.BlockSpec((tm, wblk), lambda bi, i, j: (bi * nt + i, nq + j)),
                  pl.BlockSpec((1, LANES), lambda bi, i, j: (0, 0)),
                  pl.BlockSpec((LANES, LANES), lambda bi, i, j: (0, 0))],
        out_specs=[pl.BlockSpec((1, nh, tm, LANES), lambda bi, i, j: (bi, j, i, 0)),
                   pl.BlockSpec((1, nh, tm, LANES), lambda bi, i, j: (bi, j, i, 0))],
        out_shape=[jax.ShapeDtypeStruct((b, n_heads, t, LANES), BF16),
                   jax.ShapeDtypeStruct((b, n_heads, t, LANES), BF16)],
        compiler_params=_params("arbitrary", "arbitrary", "arbitrary"),
        name="kv_post",
    )(pc, pc, g2, _ones_blockdiag())


def _attn_kernel(q_ref, k_ref, v_ref, lq1_ref, lk1_ref, lq2_ref, lk2_ref, g_ref, o_ref,
                 m_ref, l_ref, acc_ref, *, tq, tk, lam_init):
    j = pl.program_id(3)

    @pl.when(j == 0)
    def _():
        m_ref[...] = jnp.full_like(m_ref, -jnp.inf)
        l_ref[...] = jnp.zeros_like(l_ref)
        acc_ref[...] = jnp.zeros_like(acc_ref)

    q = q_ref[0, 0].reshape(2 * tq, V_HEAD_DIM)
    s = lax.dot_general(q, k_ref[0, 0], (((1,), (1,)), ((), ())), preferred_element_type=F32)
    m_prev = m_ref[...]
    m_new = jnp.maximum(m_prev, jnp.max(s, axis=-1, keepdims=True))
    alpha = jnp.exp(m_prev - m_new)
    p = jnp.exp(s - m_new)
    l_ref[...] = alpha * l_ref[...] + jnp.sum(p, axis=-1, keepdims=True)
    acc_ref[...] = alpha * acc_ref[...] + jnp.dot(p.astype(BF16), v_ref[0, 0],
                                                  preferred_element_type=F32)
    m_ref[...] = m_new

    @pl.when(j == pl.num_programs(3) - 1)
    def _():
        lam = (jnp.exp(jnp.sum(lq1_ref[...] * lk1_ref[...], axis=-1, keepdims=True))
               - jnp.exp(jnp.sum(lq2_ref[...] * lk2_ref[...], axis=-1, keepdims=True)) + lam_init)
        o12 = acc_ref[...] / l_ref[...]
        o = o12[:tq] - lam * o12[tq:]
        ms = jnp.mean(o * o, axis=-1, keepdims=True)
        y = o * lax.rsqrt(ms + 1e-6) * g_ref[...]
        o_ref[0] = (y * (1.0 - lam_init)).astype(o_ref.dtype)


def _attention(q, k, v, lq1, lk1, lq2, lk2, subln_g, lam_init):
    b, h, _, t, _ = q.shape
    l = k.shape[2]
    tq = _pick(t, (512, 256, 128))
    tk = _pick(l, (1408, 1024, 768, 512, 256, 128))
    vec = lambda a: a.reshape(1, -1)
    small = lambda n: pl.BlockSpec((1, n), lambda bi, hi, i, j: (0, 0))
    return pl.pallas_call(
        functools.partial(_attn_kernel, tq=tq, tk=tk, lam_init=lam_init),
        grid=(b, h, t // tq, l // tk),
        in_specs=[pl.BlockSpec((1, 1, 2, tq, V_HEAD_DIM), lambda bi, hi, i, j: (bi, hi, 0, i, 0)),
                  pl.BlockSpec((1, 1, tk, V_HEAD_DIM), lambda bi, hi, i, j: (bi, hi, j, 0)),
                  pl.BlockSpec((1, 1, tk, V_HEAD_DIM), lambda bi, hi, i, j: (bi, hi, j, 0)),
                  small(HEAD_DIM), small(HEAD_DIM), small(HEAD_DIM), small(HEAD_DIM),
                  small(V_HEAD_DIM)],
        out_specs=pl.BlockSpec((1, tq, V_HEAD_DIM), lambda bi, hi, i, j: (bi, i, hi)),
        out_shape=jax.ShapeDtypeStruct((b, t, h * V_HEAD_DIM), BF16),
        scratch_shapes=[pltpu.VMEM((2 * tq, 1), F32), pltpu.VMEM((2 * tq, 1), F32),
                        pltpu.VMEM((2 * tq, V_HEAD_DIM), F32)],
        compiler_params=_params("arbitrary", "arbitrary", "arbitrary", "arbitrary"),
        name="attention",
    )(q, k, v, vec(lq1), vec(lk1), vec(lq2), vec(lk2), vec(subln_g))


def _conv_kernel(ca_ref, cg_ref, cap_ref, cgp_ref, can_ref, cgn_ref, w_ref, b_ref, lg_ref, lb_ref,
                 o_ref, u_ref, y_ref, *, tm, kw, rc):
    i = pl.program_id(1)
    pad = (kw - 1) // 2
    glu = lambda a, g: a * jax.nn.sigmoid(g)
    first = i == 0
    last = i == pl.num_programs(1) - 1
    u_ref[0:HALO] = jnp.where(first, 0.0, glu(cap_ref[...], cgp_ref[...]))
    u_ref[HALO:HALO + tm] = glu(ca_ref[...], cg_ref[...])
    u_ref[HALO + tm:] = jnp.where(last, 0.0, glu(can_ref[...], cgn_ref[...]))
    nstrip = u_ref.shape[1] // LANES

    def strip(c, carry):
        lanes = pl.ds(pl.multiple_of(c * LANES, LANES), LANES)
        for r in range(tm // rc):
            acc = jnp.zeros((rc, LANES), F32)
            for k in range(kw):
                r0 = HALO - pad + r * rc + k
                acc = acc + u_ref[r0:r0 + rc, lanes] * w_ref[k:k + 1, lanes]
            y_ref[r * rc:(r + 1) * rc, lanes] = acc + b_ref[:, lanes]
        return carry

    lax.fori_loop(0, nstrip, strip, 0)
    y = y_ref[...]
    mu = jnp.mean(y, axis=-1, keepdims=True)
    yc = y - mu
    var = jnp.mean(yc * yc, axis=-1, keepdims=True)
    z = yc * lax.rsqrt(var + 1e-5) * lg_ref[...] + lb_ref[...]
    o_ref[...] = (z * jax.nn.sigmoid(z)).astype(o_ref.dtype)


def _conformer_conv(p, b, t, col_a, col_g, cw, dw_w, dw_b, ln_g, ln_b):
    kw = dw_w.shape[0]
    tm = _pick(t, (256, 128))
    nt = t // tm
    hb = tm // HALO
    nrow_h = (b * t) // HALO
    ja, jg = col_a // cw, col_g // cw
    cur = lambda jc: pl.BlockSpec((tm, cw), lambda bi, i: (bi * nt + i, jc))
    prev = lambda jc: pl.BlockSpec(
        (HALO, cw), lambda bi, i: (jnp.maximum((bi * nt + i) * hb - 1, 0), jc))
    nxt = lambda jc: pl.BlockSpec(
        (HALO, cw), lambda bi, i: (jnp.minimum((bi * nt + i + 1) * hb, nrow_h - 1), jc))
    vec = lambda n: pl.BlockSpec((1, n), lambda bi, i: (0, 0))
    return pl.pallas_call(
        functools.partial(_conv_kernel, tm=tm, kw=kw, rc=64),
        grid=(b, nt),
        in_specs=[cur(ja), cur(jg), prev(ja), prev(jg), nxt(ja), nxt(jg),
                  pl.BlockSpec((kw, cw), lambda bi, i: (0, 0)), vec(cw), vec(cw), vec(cw)],
        out_specs=pl.BlockSpec((tm, cw), lambda bi, i: (bi * nt + i, 0)),
        out_shape=jax.ShapeDtypeStruct((b * t, cw), BF16),
        scratch_shapes=[pltpu.VMEM((tm + 2 * HALO, cw), F32), pltpu.VMEM((tm, cw), F32)],
        compiler_params=_params("arbitrary", "arbitrary"),
        name="conformer_conv",
    )(p, p, p, p, p, p, dw_w, dw_b.reshape(1, cw), ln_g.reshape(1, cw), ln_b.reshape(1, cw))


def _out_proj_kernel(att_ref, conv_ref, wa_ref, wc_ref, x_ref, gate_ref, o_ref):
    mix = (jnp.dot(att_ref[...], wa_ref[...], preferred_element_type=F32)
           + jnp.dot(conv_ref[...], wc_ref[...], preferred_element_type=F32))
    o_ref[...] = x_ref[...] + gate_ref[0] * mix


def _out_proj(att, conv, w_out, x2d, gate, b, t):
    m, ka = att.shape
    d = w_out.shape[1]
    tm = _pick(t, (1024, 512, 256, 128))
    tn = _pick(d, (1024, 512, 256, 128))
    nt = t // tm
    return pl.pallas_call(
        _out_proj_kernel,
        grid=(m // tm, d // tn),
        in_specs=[pl.BlockSpec((tm, ka), lambda i, j: (i, 0)),
                  pl.BlockSpec((tm, ka), lambda i, j: (i, 0)),
                  pl.BlockSpec((ka, tn), lambda i, j: (0, j)),
                  pl.BlockSpec((ka, tn), lambda i, j: (1, j)),
                  pl.BlockSpec((tm, tn), lambda i, j: (i, j)),
                  pl.BlockSpec((1, 1, tn), lambda i, j: (i // nt, 0, j))],
        out_specs=pl.BlockSpec((tm, tn), lambda i, j: (i, j)),
        out_shape=jax.ShapeDtypeStruct((m, d), F32),
        compiler_params=_params("arbitrary", "arbitrary"),
        name="out_proj",
    )(att, conv, w_out, w_out, x2d, gate)


def _ffn_up_kernel(h_ref, hp_ref, hn_ref, wu_ref, wg_ref, cwu_ref, cwg_ref, cbu_ref, cbg_ref,
                   o_ref, *, tm, nt):
    i = pl.program_id(0)
    first = (i % nt) == 0
    last = (i % nt) == nt - 1
    rows = lax.broadcasted_iota(jnp.int32, (tm, 1), 0)

    def conv3(w_ref, cw_ref, cb_ref):
        up = jnp.dot(h_ref[...], w_ref[...], preferred_element_type=F32)
        up_prev = jnp.dot(hp_ref[...], w_ref[...], preferred_element_type=F32)[HALO - 1:HALO]
        up_next = jnp.dot(hn_ref[...], w_ref[...], preferred_element_type=F32)[0:1]
        up_prev = jnp.where(first, 0.0, up_prev)
        up_next = jnp.where(last, 0.0, up_next)
        dn = jnp.where(rows == 0, up_prev, pltpu.roll(up, 1, 0))
        nx = jnp.where(rows == tm - 1, up_next, pltpu.roll(up, tm - 1, 0))
        return dn * cw_ref[0:1] + up * cw_ref[1:2] + nx * cw_ref[2:3] + cb_ref[...]

    u = conv3(wu_ref, cwu_ref, cbu_ref)
    g = conv3(wg_ref, cwg_ref, cbg_ref)
    o_ref[...] = (g * jax.nn.sigmoid(g) * u).astype(o_ref.dtype)


def _ffn_up(h2, w_up, dw_w, dw_b, b, t):
    m, d = h2.shape
    f = w_up.shape[1] // 2
    tm = _pick(t, (1024, 512, 256, 128))
    tn = _pick(f, (256, 128))
    nt = t // tm
    hb = tm // HALO
    nrow_h = m // HALO
    nf = f // tn
    kw = dw_w.shape[0]
    return pl.pallas_call(
        functools.partial(_ffn_up_kernel, tm=tm, nt=nt),
        grid=(m // tm, nf),
        in_specs=[pl.BlockSpec((tm, d), lambda i, j: (i, 0)),
                  pl.BlockSpec((HALO, d), lambda i, j: (jnp.maximum(i * hb - 1, 0), 0)),
                  pl.BlockSpec((HALO, d), lambda i, j: (jnp.minimum((i + 1) * hb, nrow_h - 1), 0)),
                  pl.BlockSpec((d, tn), lambda i, j: (0, j)),
                  pl.BlockSpec((d, tn), lambda i, j: (0, nf + j)),
                  pl.BlockSpec((kw, tn), lambda i, j: (0, j)),
                  pl.BlockSpec((kw, tn), lambda i, j: (0, nf + j)),
                  pl.BlockSpec((1, tn), lambda i, j: (0, j)),
                  pl.BlockSpec((1, tn), lambda i, j: (0, nf + j))],
        out_specs=pl.BlockSpec((tm, tn), lambda i, j: (i, j)),
        out_shape=jax.ShapeDtypeStruct((m, f), BF16),
        compiler_params=_params("arbitrary", "arbitrary"),
        name="ffn_up",
    )(h2, h2, h2, w_up, w_up, dw_w, dw_w, dw_b.reshape(1, -1), dw_b.reshape(1, -1))


def _ffn_down_kernel(a_ref, w_ref, x_ref, gate_ref, o_ref):
    y = jnp.dot(a_ref[...], w_ref[...], preferred_element_type=F32)
    o_ref[...] = x_ref[...] + gate_ref[0] * y


def _ffn_down(act, w_down, x1, gate, t):
    m, f = act.shape
    d = w_down.shape[1]
    tm = _pick(t, (512, 256, 128))
    tn = _pick(d, (512, 256, 128))
    nt = t // tm
    return pl.pallas_call(
        _ffn_down_kernel,
        grid=(m // tm, d // tn),
        in_specs=[pl.BlockSpec((tm, f), lambda i, j: (i, 0)),
                  pl.BlockSpec((f, tn), lambda i, j: (0, j)),
                  pl.BlockSpec((tm, tn), lambda i, j: (i, j)),
                  pl.BlockSpec((1, 1, tn), lambda i, j: (i // nt, 0, j))],
        out_specs=pl.BlockSpec((tm, tn), lambda i, j: (i, j)),
        out_shape=jax.ShapeDtypeStruct((m, d), F32),
        compiler_params=_params("arbitrary", "arbitrary"),
        name="ffn_down",
    )(act, w_down, x1, gate)


def _layer(x, ctx, mods, lam_init, norm1_g, norm2_g, w_in, q_norm_g, k_norm_g, lq1, lk1, lq2, lk2,
           subln_g, conv_dw_w, conv_dw_b, conv_ln_g, conv_ln_b, w_out, w_up, ffn_dw_w, ffn_dw_b,
           w_down):
    b, t, d = x.shape
    n_ctx = ctx.shape[1]
    mix_w = w_out.shape[0]
    att_w = mix_w // 2
    conv_w = mix_w - att_w
    n_heads = att_w // V_HEAD_DIM
    qk_w = n_heads * 2 * HEAD_DIM
    v_w = n_heads * V_HEAD_DIM

    mod_x = mods[:b].reshape(b, N_MOD, 1, d)
    mod_c = jnp.broadcast_to(mods[b].reshape(1, N_MOD, 1, d), (b, N_MOD, 1, d))

    w_in_b = w_in.astype(BF16)
    h_x = _norm_mod(x, norm1_g, mod_x[:, 0], mod_x[:, 1]).reshape(b * t, d)
    h_c = _norm_mod(ctx, norm1_g, mod_c[:, 0], mod_c[:, 1]).reshape(b * n_ctx, d)

    p_x = _matmul(h_x, w_in_b, F32)
    p_c = _matmul(h_c, w_in_b, F32, col0=qk_w, ncols=qk_w + v_w)

    q, k_x, v_x = _qkv_post(p_x, b, t, n_heads, q_norm_g, k_norm_g)
    k_c, v_c = _kv_post(p_c, b, n_ctx, n_heads, k_norm_g)
    k = jnp.concatenate([k_c, k_x], axis=2)
    v = jnp.concatenate([v_c, v_x], axis=2)
    att = _attention(q, k, v, lq1, lk1, lq2, lk2, subln_g, lam_init).reshape(b * t, v_w)

    conv = _conformer_conv(p_x, b, t, 2 * qk_w + v_w, 2 * qk_w + v_w + conv_w, conv_w,
                           conv_dw_w, conv_dw_b, conv_ln_g, conv_ln_b)

    x1 = _out_proj(att, conv, w_out.astype(BF16), x.reshape(b * t, d), mod_x[:, 2], b, t)
    h2 = _norm_mod(x1.reshape(b, t, d), norm2_g, mod_x[:, 3], mod_x[:, 4]).reshape(b * t, d)
    act = _ffn_up(h2, w_up.astype(BF16), ffn_dw_w, ffn_dw_b, b, t)
    out = _ffn_down(act, w_down.astype(BF16), x1, mod_x[:, 5], t)
    return out.reshape(b, t, d)


def kernel(x, c, ctx, c_ctx, w_ada, b_ada, norm1_g, norm2_g, w_in, q_norm_g, k_norm_g, lambda_q1,
           lambda_k1, lambda_q2, lambda_k2, subln_g, conv_dw_w, conv_dw_b, conv_ln_g, conv_ln_b,
           w_out, w_up, ffn_dw_w, ffn_dw_b, w_down):
    b, t, d = x.shape
    depth = w_ada.shape[0]
    assert depth == 1, "context-stream updates are only needed for depth > 1"
    i = 0
    lam_init = 0.8 - 0.6 * math.exp(-0.3 * i)
    cc = jnp.zeros((8, d), F32).at[:b].set(c).at[b].set(c_ctx)
    mods = _ada(cc, w_ada[i], b_ada[i])
    return _layer(x, ctx, mods, lam_init, norm1_g[i], norm2_g[i], w_in[i], q_norm_g[i],
                  k_norm_g[i], lambda_q1[i], lambda_k1[i], lambda_q2[i], lambda_k2[i], subln_g[i],
                  conv_dw_w[i], conv_dw_b[i], conv_ln_g[i], conv_ln_b[i], w_out[i], w_up[i],
                  ffn_dw_w[i], ffn_dw_b[i], w_down[i])
```

```python
import functools
import math

import jax
import jax.numpy as jnp
from jax import lax
from jax.experimental import pallas as pl
from jax.experimental.pallas import tpu as pltpu

F32 = jnp.float32
BF16 = jnp.bfloat16

HEAD_DIM = 64
V_HEAD_DIM = 2 * HEAD_DIM
GRID_W = 64
ROPE_BASE = 10000.0
N_MOD = 6
LOG2E = math.log2(math.e)
FFN_ALIGN = 1024
ATTN_UNROLL = 4
LANES = 128
HALO = 16
VMEM_LIMIT = 56 * 1024 * 1024


def _params(*sem, flags=None):
    return pltpu.CompilerParams(dimension_semantics=sem, vmem_limit_bytes=VMEM_LIMIT, flags=flags)


def _pick(n, prefs):
    for p in prefs:
        if n % p == 0:
            return p
    return n


def _ada_kernel(c_ref, w_ref, b_ref, o_ref):
    c = c_ref[...]
    s = (c * jax.nn.sigmoid(c)).astype(BF16)
    o_ref[...] = jnp.dot(s, w_ref[...].astype(BF16), preferred_element_type=F32) + b_ref[...]


def _ada(cc, w_ada, b_ada):
    d, n = w_ada.shape
    tn = _pick(n, (512, 256, 128))
    return pl.pallas_call(
        _ada_kernel,
        grid=(n // tn,),
        in_specs=[pl.BlockSpec((8, d), lambda j: (0, 0)),
                  pl.BlockSpec((d, tn), lambda j: (0, j)),
                  pl.BlockSpec((1, tn), lambda j: (0, j))],
        out_specs=pl.BlockSpec((8, tn), lambda j: (0, j)),
        out_shape=jax.ShapeDtypeStruct((8, n), F32),
        compiler_params=_params("arbitrary"),
        name="ada",
    )(cc, w_ada, b_ada.reshape(1, n))


def _norm_mod_kernel(x_ref, g_ref, shift_ref, scale_ref, o_ref):
    x = x_ref[0]
    ms = jnp.mean(x * x, axis=-1, keepdims=True)
    y = x * lax.rsqrt(ms + 1e-6) * g_ref[...]
    o_ref[0] = (y * (1.0 + scale_ref[0]) + shift_ref[0]).astype(o_ref.dtype)


def _norm_mod(x, g, shift, scale):
    b, t, d = x.shape
    tm = _pick(t, (512, 256, 128))
    return pl.pallas_call(
        _norm_mod_kernel,
        grid=(b, t // tm),
        in_specs=[pl.BlockSpec((1, tm, d), lambda bi, i: (bi, i, 0)),
                  pl.BlockSpec((1, d), lambda bi, i: (0, 0)),
                  pl.BlockSpec((1, 1, d), lambda bi, i: (bi, 0, 0)),
                  pl.BlockSpec((1, 1, d), lambda bi, i: (bi, 0, 0))],
        out_specs=pl.BlockSpec((1, tm, d), lambda bi, i: (bi, i, 0)),
        out_shape=jax.ShapeDtypeStruct((b, t, d), BF16),
        compiler_params=_params("arbitrary", "arbitrary"),
        name="norm_mod",
    )(x, g.reshape(1, d), shift, scale)


def _mm_kernel(a_ref, b_ref, o_ref):
    o_ref[...] = jnp.dot(a_ref[...], b_ref[...], preferred_element_type=F32).astype(o_ref.dtype)


def _matmul(a, w, out_dtype, col0=0, ncols=None):
    m, k = a.shape
    n = w.shape[1] if ncols is None else ncols
    tm = _pick(m, (1024, 512, 256, 128))
    tn = _pick(math.gcd(n, col0) if col0 else n, (1024, 512, 256, 128))
    c0 = col0 // tn
    return pl.pallas_call(
        _mm_kernel,
        grid=(m // tm, n // tn),
        in_specs=[pl.BlockSpec((tm, k), lambda i, j: (i, 0)),
                  pl.BlockSpec((k, tn), lambda i, j: (0, c0 + j))],
        out_specs=pl.BlockSpec((tm, tn), lambda i, j: (i, j)),
        out_shape=jax.ShapeDtypeStruct((m, n), out_dtype),
        compiler_params=_params("arbitrary", "arbitrary"),
        name="matmul",
    )(a, w)


def _subhead_rms(x, g, ones_bd):
    sq = x * x
    hi = sq.astype(BF16)
    lo = (sq - hi.astype(F32)).astype(BF16)
    ss = (jnp.dot(hi, ones_bd, preferred_element_type=F32)
          + jnp.dot(lo, ones_bd, preferred_element_type=F32))
    return x * lax.rsqrt(ss * (1.0 / HEAD_DIM) + 1e-6) * g


def _rope(x, cos, sin_signed, even):
    w = x.shape[-1]
    nxt = pltpu.roll(x, w - 1, 1)
    prv = pltpu.roll(x, 1, 1)
    return x * cos + jnp.where(even, nxt, prv) * sin_signed


def _qkv_post_kernel(q_ref, k_ref, v_ref, cos_ref, sin_ref, gq_ref, gk_ref, ones_ref,
                     qo_ref, ko_ref, vo_ref, *, nh, tk):
    ones_bd = ones_ref[...]
    cos = cos_ref[...]
    sin = sin_ref[...]
    tm = cos.shape[0]
    lane = lax.broadcasted_iota(jnp.int32, cos.shape, 1)
    even = (lane % 2) == 0
    sub1 = lax.broadcasted_iota(jnp.int32, (V_HEAD_DIM, tm), 0) < HEAD_DIM
    for h in range(nh):
        sl = slice(h * V_HEAD_DIM, (h + 1) * V_HEAD_DIM)
        q = _subhead_rms(q_ref[:, sl], gq_ref[...], ones_bd)
        qt = (_rope(q, cos, sin, even) * (HEAD_DIM ** -0.5 * LOG2E)).T
        qo_ref[0, h, 0] = jnp.where(sub1, qt, 0.0).astype(BF16)
        qo_ref[0, h, 1] = jnp.where(sub1, 0.0, qt).astype(BF16)
        k = _subhead_rms(k_ref[:, sl], gk_ref[...], ones_bd)
        ko_ref[0, h] = _rope(k, cos, sin, even).astype(BF16)
        for c in range(tm // tk):
            vo_ref[0, h, c] = v_ref[c * tk:(c + 1) * tk, sl].T.astype(BF16)


def _kv_post_kernel(k_ref, v_ref, gk_ref, ones_ref, ko_ref, vo_ref, *, nh, tk):
    ones_bd = ones_ref[...]
    tm = k_ref.shape[0]
    for h in range(nh):
        sl = slice(h * V_HEAD_DIM, (h + 1) * V_HEAD_DIM)
        ko_ref[0, h] = _subhead_rms(k_ref[:, sl], gk_ref[...], ones_bd).astype(BF16)
        for c in range(tm // tk):
            vo_ref[0, h, c] = v_ref[c * tk:(c + 1) * tk, sl].T.astype(BF16)


def _ones_blockdiag():
    r = jnp.arange(LANES) // HEAD_DIM
    return (r[:, None] == r[None, :]).astype(BF16)


def _rope_tables(t):
    rows = t // GRID_W
    row = jnp.broadcast_to(jnp.arange(rows)[:, None], (rows, GRID_W)).reshape(-1)
    col = jnp.broadcast_to(jnp.arange(GRID_W)[None, :], (rows, GRID_W)).reshape(-1)
    n_pairs_axis = HEAD_DIM // 4
    freqs = ROPE_BASE ** (-jnp.arange(n_pairs_axis, dtype=F32) / n_pairs_axis)
    ang = jnp.concatenate([row.astype(F32)[:, None] * freqs,
                           col.astype(F32)[:, None] * freqs], axis=-1)
    cos = jnp.repeat(jnp.cos(ang), 2, axis=-1)
    sin = jnp.repeat(jnp.sin(ang), 2, axis=-1)
    sign = jnp.where(jnp.arange(HEAD_DIM) % 2 == 0, -1.0, 1.0).astype(F32)
    reps = LANES // HEAD_DIM
    return jnp.tile(cos, (1, reps)), jnp.tile(sin * sign, (1, reps))


def _kv_block(t, n_ctx):
    return _pick(math.gcd(t, n_ctx), (256, 128))


def _qkv_post(p, b, t, n_heads, gq, gk, tk):
    qk_w = n_heads * V_HEAD_DIM
    nh = _pick(n_heads, (4, 2, 1))
    wblk = nh * V_HEAD_DIM
    tm = _pick(t, (512, 256, 128))
    nt = t // tm
    nq = qk_w // wblk
    nc = tm // tk
    cos, sin = _rope_tables(t)
    g2 = lambda g: jnp.tile(g, LANES // HEAD_DIM).reshape(1, LANES)
    row = lambda bi, i, j: bi * nt + i
    return pl.pallas_call(
        functools.partial(_qkv_post_kernel, nh=nh, tk=tk),
        grid=(b, nt, nq),
        in_specs=[pl.BlockSpec((tm, wblk), lambda bi, i, j: (row(bi, i, j), j)),
                  pl.BlockSpec((tm, wblk), lambda bi, i, j: (row(bi, i, j), nq + j)),
                  pl.BlockSpec((tm, wblk), lambda bi, i, j: (row(bi, i, j), 2 * nq + j)),
                  pl.BlockSpec((tm, LANES), lambda bi, i, j: (i, 0)),
                  pl.BlockSpec((tm, LANES), lambda bi, i, j: (i, 0)),
                  pl.BlockSpec((1, LANES), lambda bi, i, j: (0, 0)),
                  pl.BlockSpec((1, LANES), lambda bi, i, j: (0, 0)),
                  pl.BlockSpec((LANES, LANES), lambda bi, i, j: (0, 0))],
        out_specs=[pl.BlockSpec((1, nh, 2, V_HEAD_DIM, tm), lambda bi, i, j: (bi, j, 0, 0, i)),
                   pl.BlockSpec((1, nh, tm, LANES), lambda bi, i, j: (bi, j, i, 0)),
                   pl.BlockSpec((1, nh, nc, V_HEAD_DIM, tk), lambda bi, i, j: (bi, j, i, 0, 0))],
        out_shape=[jax.ShapeDtypeStruct((b, n_heads, 2, V_HEAD_DIM, t), BF16),
                   jax.ShapeDtypeStruct((b, n_heads, t, LANES), BF16),
                   jax.ShapeDtypeStruct((b, n_heads, t // tk, V_HEAD_DIM, tk), BF16)],
        compiler_params=_params("arbitrary", "arbitrary", "arbitrary"),
        name="qkv_post",
    )(p, p, p, cos, sin, g2(gq), g2(gk), _ones_blockdiag())


def _kv_post(pc, b, t, n_heads, gk, tk):
    qk_w = n_heads * V_HEAD_DIM
    nh = _pick(n_heads, (4, 2, 1))
    wblk = nh * V_HEAD_DIM
    tm = _pick(t, (256, 128))
    nt = t // tm
    nq = qk_w // wblk
    nc = tm // tk
    g2 = jnp.tile(gk, LANES // HEAD_DIM).reshape(1, LANES)
    return pl.pallas_call(
        functools.partial(_kv_post_kernel, nh=nh, tk=tk),
        grid=(b, nt, nq),
        in_specs=[pl.BlockSpec((tm, wblk), lambda bi, i, j: (bi * nt + i, j)),
                  pl.BlockSpec((tm, wblk), lambda bi, i, j: (bi * nt + i, nq + j)),
                  pl.BlockSpec((1, LANES), lambda bi, i, j: (0, 0)),
                  pl.BlockSpec((LANES, LANES), lambda bi, i, j: (0, 0))],
        out_specs=[pl.BlockSpec((1, nh, tm, LANES), lambda bi, i, j: (bi, j, i, 0)),
                   pl.BlockSpec((1, nh, nc, V_HEAD_DIM, tk), lambda bi, i, j: (bi, j, i, 0, 0))],
        out_shape=[jax.ShapeDtypeStruct((b, n_heads, t, LANES), BF16),
                   jax.ShapeDtypeStruct((b, n_heads, t // tk, V_HEAD_DIM, tk), BF16)],
        compiler_params=_params("arbitrary", "arbitrary", "arbitrary"),
        name="kv_post",
    )(pc, pc, g2, _ones_blockdiag())


def _attn_kernel(qt_ref, k_ref, vt_ref, lq1_ref, lk1_ref, lq2_ref, lk2_ref, g_ref, o_ref,
                 q2_ref, s0_ref, s1_ref, p0_ref, p1_ref, mx0_ref, mx1_ref, a0_ref, a1_ref,
                 m_ref, l_ref, acc_ref, *, tq, tk, nblk, unroll, lam_init):
    s_refs, p_refs = (s0_ref, s1_ref), (p0_ref, p1_ref)
    mx_refs, a_refs = (mx0_ref, mx1_ref), (a0_ref, a1_ref)
    q2_ref[:, :tq] = qt_ref[0, 0, 0]
    q2_ref[:, tq:] = qt_ref[0, 0, 1]
    m_ref[...] = jnp.full_like(m_ref, -jnp.inf)
    l_ref[...] = jnp.zeros_like(l_ref)
    acc_ref[...] = jnp.zeros_like(acc_ref)

    def stage_q(t, slot):
        s = jnp.dot(k_ref[0, 0, pl.ds(pl.multiple_of(t * tk, tk), tk), :], q2_ref[...],
                    preferred_element_type=F32)
        s_refs[slot][...] = s
        mx_refs[slot][...] = jnp.max(s, axis=0, keepdims=True)

    def stage_s(slot):
        m_prev = m_ref[...]
        m_new = jnp.maximum(m_prev, mx_refs[slot][...])
        alpha = jnp.exp2(m_prev - m_new)
        p = jnp.exp2(s_refs[slot][...] - m_new)
        l_ref[...] = alpha * l_ref[...] + jnp.sum(p, axis=0, keepdims=True)
        m_ref[...] = m_new
        a_refs[slot][...] = alpha
        p_refs[slot][...] = p.astype(BF16)

    def stage_p(t, slot):
        pv = jnp.dot(vt_ref[0, 0, t], p_refs[slot][...], preferred_element_type=F32)
        acc_ref[...] = a_refs[slot][...] * acc_ref[...] + pv

    def step(t, par):
        stage_q(t, par)
        stage_s(1 - par)
        stage_p(t - 2, par)

    stage_q(0, 0)
    if nblk > 1:
        stage_q(1, 1)
    stage_s(0)
    n_steady = max(nblk - 2, 0)

    def body(i, carry):
        for u in range(unroll):
            step(2 + unroll * i + u, u % 2)
        return carry

    lax.fori_loop(0, n_steady // unroll, body, 0)
    for t in range(2 + unroll * (n_steady // unroll), nblk):
        step(t, t % 2)
    if nblk > 1:
        stage_s((nblk - 1) % 2)
        stage_p(nblk - 2, nblk % 2)
    stage_p(nblk - 1, (nblk - 1) % 2)

    lam = (jnp.exp(jnp.sum(lq1_ref[...] * lk1_ref[...], axis=-1, keepdims=True))
           - jnp.exp(jnp.sum(lq2_ref[...] * lk2_ref[...], axis=-1, keepdims=True)) + lam_init)
    o12 = acc_ref[...] * (1.0 / l_ref[...])
    o = o12[:, :tq] - lam * o12[:, tq:]
    ms = jnp.mean(o * o, axis=0, keepdims=True)
    y = o * lax.rsqrt(ms + 1e-6) * g_ref[...] * (1.0 - lam_init)
    o_ref[0] = y.T.astype(o_ref.dtype)


def _attention(qt, k, vt, lq1, lk1, lq2, lk2, subln_g, lam_init):
    b, h, _, _, t = qt.shape
    l = k.shape[2]
    nblk, _, tk = vt.shape[2:]
    tq = _pick(t, (512, 256, 128))
    vec = lambda a: a.reshape(1, -1)
    small = lambda n: pl.BlockSpec((1, n), lambda bi, hi, i: (0, 0))
    stat = pltpu.VMEM((1, 2 * tq), F32)
    return pl.pallas_call(
        functools.partial(_attn_kernel, tq=tq, tk=tk, nblk=nblk, unroll=ATTN_UNROLL,
                          lam_init=lam_init),
        grid=(b, h, t // tq),
        in_specs=[pl.BlockSpec((1, 1, 2, V_HEAD_DIM, tq), lambda bi, hi, i: (bi, hi, 0, 0, i)),
                  pl.BlockSpec((1, 1, l, V_HEAD_DIM), lambda bi, hi, i: (bi, hi, 0, 0)),
                  pl.BlockSpec((1, 1, nblk, V_HEAD_DIM, tk), lambda bi, hi, i: (bi, hi, 0, 0, 0)),
                  small(HEAD_DIM), small(HEAD_DIM), small(HEAD_DIM), small(HEAD_DIM),
                  pl.BlockSpec((V_HEAD_DIM, 1), lambda bi, hi, i: (0, 0))],
        out_specs=pl.BlockSpec((1, tq, V_HEAD_DIM), lambda bi, hi, i: (bi, i, hi)),
        out_shape=jax.ShapeDtypeStruct((b, t, h * V_HEAD_DIM), BF16),
        scratch_shapes=[pltpu.VMEM((V_HEAD_DIM, 2 * tq), BF16),
                        pltpu.VMEM((tk, 2 * tq), F32), pltpu.VMEM((tk, 2 * tq), F32),
                        pltpu.VMEM((tk, 2 * tq), BF16), pltpu.VMEM((tk, 2 * tq), BF16),
                        stat, stat, stat, stat, stat, stat,
                        pltpu.VMEM((V_HEAD_DIM, 2 * tq), F32)],
        compiler_params=_params("arbitrary", "arbitrary", "arbitrary"),
        name="attention",
    )(qt, k, vt, vec(lq1), vec(lk1), vec(lq2), vec(lk2), subln_g.reshape(-1, 1))


def _conv_kernel(ca_ref, cg_ref, cap_ref, cgp_ref, can_ref, cgn_ref, w_ref, b_ref, lg_ref, lb_ref,
                 o_ref, u_ref, y_ref, *, tm, kw, rc):
    i = pl.program_id(1)
    pad = (kw - 1) // 2
    glu = lambda a, g: a * jax.nn.sigmoid(g)
    first = i == 0
    last = i == pl.num_programs(1) - 1
    u_ref[0:HALO] = jnp.where(first, 0.0, glu(cap_ref[...], cgp_ref[...]))
    u_ref[HALO:HALO + tm] = glu(ca_ref[...], cg_ref[...])
    u_ref[HALO + tm:] = jnp.where(last, 0.0, glu(can_ref[...], cgn_ref[...]))
    nstrip = u_ref.shape[1] // LANES

    def strip(c, carry):
        lanes = pl.ds(pl.multiple_of(c * LANES, LANES), LANES)
        for r in range(tm // rc):
            acc = jnp.zeros((rc, LANES), F32)
            for k in range(kw):
                r0 = HALO - pad + r * rc + k
                acc = acc + u_ref[r0:r0 + rc, lanes] * w_ref[k:k + 1, lanes]
            y_ref[r * rc:(r + 1) * rc, lanes] = acc + b_ref[:, lanes]
        return carry

    lax.fori_loop(0, nstrip, strip, 0)
    y = y_ref[...]
    mu = jnp.mean(y, axis=-1, keepdims=True)
    yc = y - mu
    var = jnp.mean(yc * yc, axis=-1, keepdims=True)
    z = yc * lax.rsqrt(var + 1e-5) * lg_ref[...] + lb_ref[...]
    o_ref[...] = (z * jax.nn.sigmoid(z)).astype(o_ref.dtype)


def _conformer_conv(p, b, t, col_a, col_g, cw, dw_w, dw_b, ln_g, ln_b):
    kw = dw_w.shape[0]
    tm = _pick(t, (256, 128))
    nt = t // tm
    hb = tm // HALO
    nrow_h = (b * t) // HALO
    ja, jg = col_a // cw, col_g // cw
    cur = lambda jc: pl.BlockSpec((tm, cw), lambda bi, i: (bi * nt + i, jc))
    prev = lambda jc: pl.BlockSpec(
        (HALO, cw), lambda bi, i: (jnp.maximum((bi * nt + i) * hb - 1, 0), jc))
    nxt = lambda jc: pl.BlockSpec(
        (HALO, cw), lambda bi, i: (jnp.minimum((bi * nt + i + 1) * hb, nrow_h - 1), jc))
    vec = lambda n: pl.BlockSpec((1, n), lambda bi, i: (0, 0))
    return pl.pallas_call(
        functools.partial(_conv_kernel, tm=tm, kw=kw, rc=64),
        grid=(b, nt),
        in_specs=[cur(ja), cur(jg), prev(ja), prev(jg), nxt(ja), nxt(jg),
                  pl.BlockSpec((kw, cw), lambda bi, i: (0, 0)), vec(cw), vec(cw), vec(cw)],
        out_specs=pl.BlockSpec((tm, cw), lambda bi, i: (bi * nt + i, 0)),
        out_shape=jax.ShapeDtypeStruct((b * t, cw), BF16),
        scratch_shapes=[pltpu.VMEM((tm + 2 * HALO, cw), F32), pltpu.VMEM((tm, cw), F32)],
        compiler_params=_params("arbitrary", "arbitrary"),
        name="conformer_conv",
    )(p, p, p, p, p, p, dw_w, dw_b.reshape(1, cw), ln_g.reshape(1, cw), ln_b.reshape(1, cw))


def _out_proj_kernel(att_ref, conv_ref, wa_ref, wc_ref, x_ref, gate_ref, o_ref):
    mix = (jnp.dot(att_ref[...], wa_ref[...], preferred_element_type=F32)
           + jnp.dot(conv_ref[...], wc_ref[...], preferred_element_type=F32))
    o_ref[...] = x_ref[...] + gate_ref[0] * mix


def _out_proj(att, conv, w_out, x2d, gate, b, t):
    m, ka = att.shape
    d = w_out.shape[1]
    tm = _pick(t, (1024, 512, 256, 128))
    tn = _pick(d, (1024, 512, 256, 128))
    nt = t // tm
    return pl.pallas_call(
        _out_proj_kernel,
        grid=(m // tm, d // tn),
        in_specs=[pl.BlockSpec((tm, ka), lambda i, j: (i, 0)),
                  pl.BlockSpec((tm, ka), lambda i, j: (i, 0)),
                  pl.BlockSpec((ka, tn), lambda i, j: (0, j)),
                  pl.BlockSpec((ka, tn), lambda i, j: (1, j)),
                  pl.BlockSpec((tm, tn), lambda i, j: (i, j)),
                  pl.BlockSpec((1, 1, tn), lambda i, j: (i // nt, 0, j))],
        out_specs=pl.BlockSpec((tm, tn), lambda i, j: (i, j)),
        out_shape=jax.ShapeDtypeStruct((m, d), F32),
        compiler_params=_params("arbitrary", "arbitrary"),
        name="out_proj",
    )(att, conv, w_out, w_out, x2d, gate)


def _ffn_up_kernel(h_ref, hp_ref, hn_ref, wu_ref, wg_ref, cwu_ref, cwg_ref, cbu_ref, cbg_ref,
                   o_ref, a_ref, *, tm, nt, tc):
    i = pl.program_id(0)

    @pl.when(pl.program_id(1) == 0)
    def _():
        first = (i % nt) == 0
        last = (i % nt) == nt - 1
        a_ref[0:HALO] = jnp.where(first, jnp.zeros_like(hp_ref), hp_ref[...])
        a_ref[HALO:HALO + tm] = h_ref[...]
        a_ref[HALO + tm:] = jnp.where(last, jnp.zeros_like(hn_ref), hn_ref[...])

    ext = tm + 2 * HALO

    def conv3(w_ref, cw_ref, cb_ref, cols):
        up = jnp.dot(a_ref[...], w_ref[:, cols], preferred_element_type=F32)
        dn = pltpu.roll(up, 1, 0)[HALO:HALO + tm]
        nx = pltpu.roll(up, ext - 1, 0)[HALO:HALO + tm]
        return (dn * cw_ref[0:1, cols] + up[HALO:HALO + tm] * cw_ref[1:2, cols]
                + nx * cw_ref[2:3, cols] + cb_ref[:, cols])

    for c in range(o_ref.shape[1] // tc):
        cols = slice(c * tc, (c + 1) * tc)
        u = conv3(wu_ref, cwu_ref, cbu_ref, cols)
        g = conv3(wg_ref, cwg_ref, cbg_ref, cols)
        o_ref[:, cols] = (g * jax.nn.sigmoid(g) * u).astype(o_ref.dtype)


def _ffn_up(h2, w_up, dw_w, dw_b, b, t):
    m, d = h2.shape
    f = w_up.shape[1] // 2
    tm = _pick(t, (1024, 512, 256, 128))
    tn = _pick(f, (512, 256, 128))
    tc = _pick(tn, (256, 128))
    nt = t // tm
    hb = tm // HALO
    nrow_h = m // HALO
    nf = f // tn
    kw = dw_w.shape[0]
    return pl.pallas_call(
        functools.partial(_ffn_up_kernel, tm=tm, nt=nt, tc=tc),
        grid=(m // tm, nf),
        in_specs=[pl.BlockSpec((tm, d), lambda i, j: (i, 0)),
                  pl.BlockSpec((HALO, d), lambda i, j: (jnp.maximum(i * hb - 1, 0), 0)),
                  pl.BlockSpec((HALO, d), lambda i, j: (jnp.minimum((i + 1) * hb, nrow_h - 1), 0)),
                  pl.BlockSpec((d, tn), lambda i, j: (0, j)),
                  pl.BlockSpec((d, tn), lambda i, j: (0, nf + j)),
                  pl.BlockSpec((kw, tn), lambda i, j: (0, j)),
                  pl.BlockSpec((kw, tn), lambda i, j: (0, nf + j)),
                  pl.BlockSpec((1, tn), lambda i, j: (0, j)),
                  pl.BlockSpec((1, tn), lambda i, j: (0, nf + j))],
        out_specs=pl.BlockSpec((tm, tn), lambda i, j: (i, j)),
        out_shape=jax.ShapeDtypeStruct((m, f), BF16),
        scratch_shapes=[pltpu.VMEM((tm + 2 * HALO, d), BF16)],
        compiler_params=_params("arbitrary", "arbitrary"),
        name="ffn_up",
    )(h2, h2, h2, w_up, w_up, dw_w, dw_w, dw_b.reshape(1, -1), dw_b.reshape(1, -1))


def _pad_ffn(w_up, dw_w, dw_b, w_down):
    f = w_down.shape[0]
    fp = -(-f // FFN_ALIGN) * FFN_ALIGN
    pad2 = lambda a: jnp.concatenate(
        [jnp.pad(a[..., :f], ((0, 0), (0, fp - f))), jnp.pad(a[..., f:], ((0, 0), (0, fp - f)))],
        axis=-1)
    return (pad2(w_up.astype(BF16)), pad2(dw_w), pad2(dw_b.reshape(1, -1)).reshape(-1),
            jnp.pad(w_down.astype(BF16), ((0, fp - f), (0, 0))))


def _ffn_down_kernel(a_ref, w_ref, x_ref, gate_ref, o_ref):
    y = jnp.dot(a_ref[...], w_ref[...], preferred_element_type=F32)
    o_ref[...] = x_ref[...] + gate_ref[0] * y


def _ffn_down(act, w_down, x1, gate, t):
    m, f = act.shape
    d = w_down.shape[1]
    tm = _pick(t, (512, 256, 128))
    tn = _pick(d, (512, 256, 128))
    nt = t // tm
    return pl.pallas_call(
        _ffn_down_kernel,
        grid=(m // tm, d // tn),
        in_specs=[pl.BlockSpec((tm, f), lambda i, j: (i, 0)),
                  pl.BlockSpec((f, tn), lambda i, j: (0, j)),
                  pl.BlockSpec((tm, tn), lambda i, j: (i, j)),
                  pl.BlockSpec((1, 1, tn), lambda i, j: (i // nt, 0, j))],
        out_specs=pl.BlockSpec((tm, tn), lambda i, j: (i, j)),
        out_shape=jax.ShapeDtypeStruct((m, d), F32),
        compiler_params=_params("arbitrary", "arbitrary"),
        name="ffn_down",
    )(act, w_down, x1, gate)


def _layer(x, ctx, mods, lam_init, norm1_g, norm2_g, w_in, q_norm_g, k_norm_g, lq1, lk1, lq2, lk2,
           subln_g, conv_dw_w, conv_dw_b, conv_ln_g, conv_ln_b, w_out, w_up, ffn_dw_w, ffn_dw_b,
           w_down):
    b, t, d = x.shape
    n_ctx = ctx.shape[1]
    mix_w = w_out.shape[0]
    att_w = mix_w // 2
    conv_w = mix_w - att_w
    n_heads = att_w // V_HEAD_DIM
    qk_w = n_heads * 2 * HEAD_DIM
    v_w = n_heads * V_HEAD_DIM

    mod_x = mods[:b].reshape(b, N_MOD, 1, d)
    mod_c = jnp.broadcast_to(mods[b].reshape(1, N_MOD, 1, d), (b, N_MOD, 1, d))

    w_in_b = w_in.astype(BF16)
    h_x = _norm_mod(x, norm1_g, mod_x[:, 0], mod_x[:, 1]).reshape(b * t, d)
    h_c = _norm_mod(ctx, norm1_g, mod_c[:, 0], mod_c[:, 1]).reshape(b * n_ctx, d)

    p_x = _matmul(h_x, w_in_b, F32)
    p_c = _matmul(h_c, w_in_b, F32, col0=qk_w, ncols=qk_w + v_w)

    tk = _kv_block(t, n_ctx)
    qt, k_x, vt_x = _qkv_post(p_x, b, t, n_heads, q_norm_g, k_norm_g, tk)
    k_c, vt_c = _kv_post(p_c, b, n_ctx, n_heads, k_norm_g, tk)
    k = jnp.concatenate([k_c, k_x], axis=2)
    vt = jnp.concatenate([vt_c, vt_x], axis=2)
    att = _attention(qt, k, vt, lq1, lk1, lq2, lk2, subln_g, lam_init).reshape(b * t, v_w)

    conv = _conformer_conv(p_x, b, t, 2 * qk_w + v_w, 2 * qk_w + v_w + conv_w, conv_w,
                           conv_dw_w, conv_dw_b, conv_ln_g, conv_ln_b)

    x1 = _out_proj(att, conv, w_out.astype(BF16), x.reshape(b * t, d), mod_x[:, 2], b, t)
    h2 = _norm_mod(x1.reshape(b, t, d), norm2_g, mod_x[:, 3], mod_x[:, 4]).reshape(b * t, d)
    w_up_p, dw_w_p, dw_b_p, w_down_p = _pad_ffn(w_up, ffn_dw_w, ffn_dw_b, w_down)
    act = _ffn_up(h2, w_up_p, dw_w_p, dw_b_p, b, t)
    out = _ffn_down(act, w_down_p, x1, mod_x[:, 5], t)
    return out.reshape(b, t, d)


def kernel(x, c, ctx, c_ctx, w_ada, b_ada, norm1_g, norm2_g, w_in, q_norm_g, k_norm_g, lambda_q1,
           lambda_k1, lambda_q2, lambda_k2, subln_g, conv_dw_w, conv_dw_b, conv_ln_g, conv_ln_b,
           w_out, w_up, ffn_dw_w, ffn_dw_b, w_down):
    b, t, d = x.shape
    depth = w_ada.shape[0]
    assert depth == 1, "context-stream updates are only needed for depth > 1"
    i = 0
    lam_init = 0.8 - 0.6 * math.exp(-0.3 * i)
    cc = jnp.zeros((8, d), F32).at[:b].set(c).at[b].set(c_ctx)
    mods = _ada(cc, w_ada[i], b_ada[i])
    return _layer(x, ctx, mods, lam_init, norm1_g[i], norm2_g[i], w_in[i], q_norm_g[i],
                  k_norm_g[i], lambda_q1[i], lambda_k1[i], lambda_q2[i], lambda_k2[i], subln_g[i],
                  conv_dw_w[i], conv_dw_b[i], conv_ln_g[i], conv_ln_b[i], w_out[i], w_up[i],
                  ffn_dw_w[i], ffn_dw_b[i], w_down[i])
```

```python
import functools
import math

import jax
import jax.numpy as jnp
from jax import lax
from jax.experimental import pallas as pl
from jax.experimental.pallas import tpu as pltpu

F32 = jnp.float32
BF16 = jnp.bfloat16

HEAD_DIM = 64
V_HEAD_DIM = 2 * HEAD_DIM
GRID_W = 64
ROPE_BASE = 10000.0
N_MOD = 6
LOG2E = math.log2(math.e)
FFN_ALIGN = 1024
LANES = 128
HALO = 16
VMEM_LIMIT = 56 * 1024 * 1024


def _params(*sem, flags=None):
    return pltpu.CompilerParams(dimension_semantics=sem, vmem_limit_bytes=VMEM_LIMIT, flags=flags)


def _pick(n, prefs):
    for p in prefs:
        if n % p == 0:
            return p
    return n


def _ada_kernel(c_ref, w_ref, b_ref, o_ref):
    c = c_ref[...]
    s = (c * jax.nn.sigmoid(c)).astype(BF16)
    o_ref[...] = jnp.dot(s, w_ref[...].astype(BF16), preferred_element_type=F32) + b_ref[...]


def _ada(cc, w_ada, b_ada):
    d, n = w_ada.shape
    tn = _pick(n, (512, 256, 128))
    return pl.pallas_call(
        _ada_kernel,
        grid=(n // tn,),
        in_specs=[pl.BlockSpec((8, d), lambda j: (0, 0)),
                  pl.BlockSpec((d, tn), lambda j: (0, j)),
                  pl.BlockSpec((1, tn), lambda j: (0, j))],
        out_specs=pl.BlockSpec((8, tn), lambda j: (0, j)),
        out_shape=jax.ShapeDtypeStruct((8, n), F32),
        compiler_params=_params("arbitrary"),
        name="ada",
    )(cc, w_ada, b_ada.reshape(1, n))


def _norm_mod_kernel(x_ref, g_ref, shift_ref, scale_ref, o_ref):
    x = x_ref[0]
    ms = jnp.mean(x * x, axis=-1, keepdims=True)
    y = x * lax.rsqrt(ms + 1e-6) * g_ref[...]
    o_ref[0] = (y * (1.0 + scale_ref[0]) + shift_ref[0]).astype(o_ref.dtype)


def _norm_mod(x, g, shift, scale):
    b, t, d = x.shape
    tm = _pick(t, (512, 256, 128))
    return pl.pallas_call(
        _norm_mod_kernel,
        grid=(b, t // tm),
        in_specs=[pl.BlockSpec((1, tm, d), lambda bi, i: (bi, i, 0)),
                  pl.BlockSpec((1, d), lambda bi, i: (0, 0)),
                  pl.BlockSpec((1, 1, d), lambda bi, i: (bi, 0, 0)),
                  pl.BlockSpec((1, 1, d), lambda bi, i: (bi, 0, 0))],
        out_specs=pl.BlockSpec((1, tm, d), lambda bi, i: (bi, i, 0)),
        out_shape=jax.ShapeDtypeStruct((b, t, d), BF16),
        compiler_params=_params("arbitrary", "arbitrary"),
        name="norm_mod",
    )(x, g.reshape(1, d), shift, scale)


def _mm_kernel(a_ref, b_ref, o_ref):
    o_ref[...] = jnp.dot(a_ref[...], b_ref[...], preferred_element_type=F32).astype(o_ref.dtype)


def _matmul(a, w, out_dtype, col0=0, ncols=None):
    m, k = a.shape
    n = w.shape[1] if ncols is None else ncols
    tm = _pick(m, (1024, 512, 256, 128))
    tn = _pick(math.gcd(n, col0) if col0 else n, (1024, 512, 256, 128))
    c0 = col0 // tn
    return pl.pallas_call(
        _mm_kernel,
        grid=(m // tm, n // tn),
        in_specs=[pl.BlockSpec((tm, k), lambda i, j: (i, 0)),
                  pl.BlockSpec((k, tn), lambda i, j: (0, c0 + j))],
        out_specs=pl.BlockSpec((tm, tn), lambda i, j: (i, j)),
        out_shape=jax.ShapeDtypeStruct((m, n), out_dtype),
        compiler_params=_params("arbitrary", "arbitrary"),
        name="matmul",
    )(a, w)


def _subhead_rms(x, g, ones_bd):
    sq = x * x
    hi = sq.astype(BF16)
    lo = (sq - hi.astype(F32)).astype(BF16)
    ss = (jnp.dot(hi, ones_bd, preferred_element_type=F32)
          + jnp.dot(lo, ones_bd, preferred_element_type=F32))
    return x * lax.rsqrt(ss * (1.0 / HEAD_DIM) + 1e-6) * g


def _rope(x, cos, sin_signed, even):
    w = x.shape[-1]
    nxt = pltpu.roll(x, w - 1, 1)
    prv = pltpu.roll(x, 1, 1)
    return x * cos + jnp.where(even, nxt, prv) * sin_signed


def _qkv_post_kernel(q_ref, k_ref, v_ref, cos_ref, sin_ref, gq_ref, gk_ref, ones_ref,
                     qo_ref, ko_ref, vo_ref, *, nh, tk):
    ones_bd = ones_ref[...]
    cos = cos_ref[...]
    sin = sin_ref[...]
    tm = cos.shape[0]
    lane = lax.broadcasted_iota(jnp.int32, cos.shape, 1)
    even = (lane % 2) == 0
    sub1 = lax.broadcasted_iota(jnp.int32, (V_HEAD_DIM, tm), 0) < HEAD_DIM
    for h in range(nh):
        sl = slice(h * V_HEAD_DIM, (h + 1) * V_HEAD_DIM)
        q = _subhead_rms(q_ref[:, sl], gq_ref[...], ones_bd)
        qt = (_rope(q, cos, sin, even) * (HEAD_DIM ** -0.5 * LOG2E)).T
        qo_ref[0, h, 0] = jnp.where(sub1, qt, 0.0).astype(BF16)
        qo_ref[0, h, 1] = jnp.where(sub1, 0.0, qt).astype(BF16)
        k = _subhead_rms(k_ref[:, sl], gk_ref[...], ones_bd)
        ko_ref[0, h] = _rope(k, cos, sin, even).astype(BF16)
        for c in range(tm // tk):
            vo_ref[0, h, c] = v_ref[c * tk:(c + 1) * tk, sl].T.astype(BF16)


def _kv_post_kernel(k_ref, v_ref, gk_ref, ones_ref, ko_ref, vo_ref, *, nh, tk):
    ones_bd = ones_ref[...]
    tm = k_ref.shape[0]
    for h in range(nh):
        sl = slice(h * V_HEAD_DIM, (h + 1) * V_HEAD_DIM)
        ko_ref[0, h] = _subhead_rms(k_ref[:, sl], gk_ref[...], ones_bd).astype(BF16)
        for c in range(tm // tk):
            vo_ref[0, h, c] = v_ref[c * tk:(c + 1) * tk, sl].T.astype(BF16)


def _ones_blockdiag():
    r = jnp.arange(LANES) // HEAD_DIM
    return (r[:, None] == r[None, :]).astype(BF16)


def _rope_tables(t):
    rows = t // GRID_W
    row = jnp.broadcast_to(jnp.arange(rows)[:, None], (rows, GRID_W)).reshape(-1)
    col = jnp.broadcast_to(jnp.arange(GRID_W)[None, :], (rows, GRID_W)).reshape(-1)
    n_pairs_axis = HEAD_DIM // 4
    freqs = ROPE_BASE ** (-jnp.arange(n_pairs_axis, dtype=F32) / n_pairs_axis)
    ang = jnp.concatenate([row.astype(F32)[:, None] * freqs,
                           col.astype(F32)[:, None] * freqs], axis=-1)
    cos = jnp.repeat(jnp.cos(ang), 2, axis=-1)
    sin = jnp.repeat(jnp.sin(ang), 2, axis=-1)
    sign = jnp.where(jnp.arange(HEAD_DIM) % 2 == 0, -1.0, 1.0).astype(F32)
    reps = LANES // HEAD_DIM
    return jnp.tile(cos, (1, reps)), jnp.tile(sin * sign, (1, reps))


def _kv_block(t, n_ctx):
    return _pick(math.gcd(t, n_ctx), (256, 128))


def _qkv_post(p, b, t, n_heads, gq, gk, tk):
    qk_w = n_heads * V_HEAD_DIM
    nh = _pick(n_heads, (4, 2, 1))
    wblk = nh * V_HEAD_DIM
    tm = _pick(t, (512, 256, 128))
    nt = t // tm
    nq = qk_w // wblk
    nc = tm // tk
    cos, sin = _rope_tables(t)
    g2 = lambda g: jnp.tile(g, LANES // HEAD_DIM).reshape(1, LANES)
    row = lambda bi, i, j: bi * nt + i
    return pl.pallas_call(
        functools.partial(_qkv_post_kernel, nh=nh, tk=tk),
        grid=(b, nt, nq),
        in_specs=[pl.BlockSpec((tm, wblk), lambda bi, i, j: (row(bi, i, j), j)),
                  pl.BlockSpec((tm, wblk), lambda bi, i, j: (row(bi, i, j), nq + j)),
                  pl.BlockSpec((tm, wblk), lambda bi, i, j: (row(bi, i, j), 2 * nq + j)),
                  pl.BlockSpec((tm, LANES), lambda bi, i, j: (i, 0)),
                  pl.BlockSpec((tm, LANES), lambda bi, i, j: (i, 0)),
                  pl.BlockSpec((1, LANES), lambda bi, i, j: (0, 0)),
                  pl.BlockSpec((1, LANES), lambda bi, i, j: (0, 0)),
                  pl.BlockSpec((LANES, LANES), lambda bi, i, j: (0, 0))],
        out_specs=[pl.BlockSpec((1, nh, 2, V_HEAD_DIM, tm), lambda bi, i, j: (bi, j, 0, 0, i)),
                   pl.BlockSpec((1, nh, tm, LANES), lambda bi, i, j: (bi, j, i, 0)),
                   pl.BlockSpec((1, nh, nc, V_HEAD_DIM, tk), lambda bi, i, j: (bi, j, i, 0, 0))],
        out_shape=[jax.ShapeDtypeStruct((b, n_heads, 2, V_HEAD_DIM, t), BF16),
                   jax.ShapeDtypeStruct((b, n_heads, t, LANES), BF16),
                   jax.ShapeDtypeStruct((b, n_heads, t // tk, V_HEAD_DIM, tk), BF16)],
        compiler_params=_params("arbitrary", "arbitrary", "arbitrary"),
        name="qkv_post",
    )(p, p, p, cos, sin, g2(gq), g2(gk), _ones_blockdiag())


def _kv_post(pc, b, t, n_heads, gk, tk):
    qk_w = n_heads * V_HEAD_DIM
    nh = _pick(n_heads, (4, 2, 1))
    wblk = nh * V_HEAD_DIM
    tm = _pick(t, (256, 128))
    nt = t // tm
    nq = qk_w // wblk
    nc = tm // tk
    g2 = jnp.tile(gk, LANES // HEAD_DIM).reshape(1, LANES)
    return pl.pallas_call(
        functools.partial(_kv_post_kernel, nh=nh, tk=tk),
        grid=(b, nt, nq),
        in_specs=[pl.BlockSpec((tm, wblk), lambda bi, i, j: (bi * nt + i, j)),
                  pl.BlockSpec((tm, wblk), lambda bi, i, j: (bi * nt + i, nq + j)),
                  pl.BlockSpec((1, LANES), lambda bi, i, j: (0, 0)),
                  pl.BlockSpec((LANES, LANES), lambda bi, i, j: (0, 0))],
        out_specs=[pl.BlockSpec((1, nh, tm, LANES), lambda bi, i, j: (bi, j, i, 0)),
                   pl.BlockSpec((1, nh, nc, V_HEAD_DIM, tk), lambda bi, i, j: (bi, j, i, 0, 0))],
        out_shape=[jax.ShapeDtypeStruct((b, n_heads, t, LANES), BF16),
                   jax.ShapeDtypeStruct((b, n_heads, t // tk, V_HEAD_DIM, tk), BF16)],
        compiler_params=_params("arbitrary", "arbitrary", "arbitrary"),
        name="kv_post",
    )(pc, pc, g2, _ones_blockdiag())


def _attn_kernel(qt_ref, k_ref, vt_ref, lq1_ref, lk1_ref, lq2_ref, lk2_ref, g_ref, o_ref,
                 q2_ref, s_ref, p_ref, mx_ref, a_ref, m_ref, l_ref, acc_ref,
                 *, tq, tkw, sub, nstep, lam_init):
    tk = tkw * sub
    nw = 2 * tq
    cw = _pick(nw, (2 * LANES, LANES))
    chunks = [slice(c * cw, (c + 1) * cw) for c in range(nw // cw)]
    q2_ref[:, :tq] = qt_ref[0, 0, 0]
    q2_ref[:, tq:] = qt_ref[0, 0, 1]
    m_ref[...] = jnp.full_like(m_ref, -jnp.inf)
    l_ref[...] = jnp.zeros_like(l_ref)
    acc_ref[...] = jnp.zeros_like(acc_ref)

    def stage_p(t):
        for cols in chunks:
            pv = None
            for i in range(sub):
                d = jnp.dot(vt_ref[0, 0, t * sub + i], p_ref[i * tkw:(i + 1) * tkw, cols],
                            preferred_element_type=F32)
                pv = d if pv is None else pv + d
            acc_ref[:, cols] = a_ref[:, cols] * acc_ref[:, cols] + pv

    def stage_s():
        m_prev = m_ref[...]
        m_new = jnp.maximum(m_prev, mx_ref[...])
        alpha = jnp.exp2(m_prev - m_new)
        for cols in chunks:
            p = jnp.exp2(s_ref[:, cols] - m_new[:, cols])
            l_ref[:, cols] = alpha[:, cols] * l_ref[:, cols] + jnp.sum(p, axis=0, keepdims=True)
            p_ref[:, cols] = p.astype(BF16)
        m_ref[...] = m_new
        a_ref[...] = alpha

    def stage_q(t):
        kb = k_ref[0, 0, pl.ds(pl.multiple_of(t * tk, tk), tk), :]
        for cols in chunks:
            s = jnp.dot(kb, q2_ref[:, cols], preferred_element_type=F32)
            s_ref[:, cols] = s
            mx_ref[:, cols] = jnp.max(s, axis=0, keepdims=True)

    def step(t, carry):
        stage_p(t - 2)
        stage_s()
        stage_q(t)
        return carry

    stage_q(0)
    stage_s()
    if nstep > 1:
        stage_q(1)
        lax.fori_loop(2, nstep, step, 0)
        stage_p(nstep - 2)
        stage_s()
    stage_p(nstep - 1)

    lam = (jnp.exp(jnp.sum(lq1_ref[...] * lk1_ref[...], axis=-1, keepdims=True))
           - jnp.exp(jnp.sum(lq2_ref[...] * lk2_ref[...], axis=-1, keepdims=True)) + lam_init)
    o12 = acc_ref[...] * (1.0 / l_ref[...])
    o = o12[:, :tq] - lam * o12[:, tq:]
    ms = jnp.mean(o * o, axis=0, keepdims=True)
    y = o * lax.rsqrt(ms + 1e-6) * g_ref[...] * (1.0 - lam_init)
    o_ref[0] = y.T.astype(o_ref.dtype)


def _attention(qt, k, vt, lq1, lk1, lq2, lk2, subln_g, lam_init):
    b, h, _, _, t = qt.shape
    l = k.shape[2]
    nblk, _, tkw = vt.shape[2:]
    sub = _pick(nblk, (3, 2, 1))
    tk = sub * tkw
    tq = _pick(t, (512, 256, 128))
    vec = lambda a: a.reshape(1, -1)
    small = lambda n: pl.BlockSpec((1, n), lambda bi, hi, i: (0, 0))
    stat = pltpu.VMEM((1, 2 * tq), F32)
    return pl.pallas_call(
        functools.partial(_attn_kernel, tq=tq, tkw=tkw, sub=sub, nstep=nblk // sub,
                          lam_init=lam_init),
        grid=(b, h, t // tq),
        in_specs=[pl.BlockSpec((1, 1, 2, V_HEAD_DIM, tq), lambda bi, hi, i: (bi, hi, 0, 0, i)),
                  pl.BlockSpec((1, 1, l, V_HEAD_DIM), lambda bi, hi, i: (bi, hi, 0, 0)),
                  pl.BlockSpec((1, 1, nblk, V_HEAD_DIM, tkw), lambda bi, hi, i: (bi, hi, 0, 0, 0)),
                  small(HEAD_DIM), small(HEAD_DIM), small(HEAD_DIM), small(HEAD_DIM),
                  pl.BlockSpec((V_HEAD_DIM, 1), lambda bi, hi, i: (0, 0))],
        out_specs=pl.BlockSpec((1, tq, V_HEAD_DIM), lambda bi, hi, i: (bi, i, hi)),
        out_shape=jax.ShapeDtypeStruct((b, t, h * V_HEAD_DIM), BF16),
        scratch_shapes=[pltpu.VMEM((V_HEAD_DIM, 2 * tq), BF16),
                        pltpu.VMEM((tk, 2 * tq), F32), pltpu.VMEM((tk, 2 * tq), BF16),
                        stat, stat, stat, stat,
                        pltpu.VMEM((V_HEAD_DIM, 2 * tq), F32)],
        compiler_params=_params("arbitrary", "arbitrary", "arbitrary"),
        name="attention",
    )(qt, k, vt, vec(lq1), vec(lk1), vec(lq2), vec(lk2), subln_g.reshape(-1, 1))


def _conv_kernel(ca_ref, cg_ref, cap_ref, cgp_ref, can_ref, cgn_ref, w_ref, b_ref, lg_ref, lb_ref,
                 o_ref, u_ref, y_ref, *, tm, kw, rc):
    i = pl.program_id(1)
    pad = (kw - 1) // 2
    glu = lambda a, g: a * jax.nn.sigmoid(g)
    first = i == 0
    last = i == pl.num_programs(1) - 1
    u_ref[0:HALO] = jnp.where(first, 0.0, glu(cap_ref[...], cgp_ref[...]))
    u_ref[HALO:HALO + tm] = glu(ca_ref[...], cg_ref[...])
    u_ref[HALO + tm:] = jnp.where(last, 0.0, glu(can_ref[...], cgn_ref[...]))
    nstrip = u_ref.shape[1] // LANES

    def strip(c, carry):
        lanes = pl.ds(pl.multiple_of(c * LANES, LANES), LANES)
        for r in range(tm // rc):
            acc = jnp.zeros((rc, LANES), F32)
            for k in range(kw):
                r0 = HALO - pad + r * rc + k
                acc = acc + u_ref[r0:r0 + rc, lanes] * w_ref[k:k + 1, lanes]
            y_ref[r * rc:(r + 1) * rc, lanes] = acc + b_ref[:, lanes]
        return carry

    lax.fori_loop(0, nstrip, strip, 0)
    y = y_ref[...]
    mu = jnp.mean(y, axis=-1, keepdims=True)
    yc = y - mu
    var = jnp.mean(yc * yc, axis=-1, keepdims=True)
    z = yc * lax.rsqrt(var + 1e-5) * lg_ref[...] + lb_ref[...]
    o_ref[...] = (z * jax.nn.sigmoid(z)).astype(o_ref.dtype)


def _conformer_conv(p, b, t, col_a, col_g, cw, dw_w, dw_b, ln_g, ln_b):
    kw = dw_w.shape[0]
    tm = _pick(t, (256, 128))
    nt = t // tm
    hb = tm // HALO
    nrow_h = (b * t) // HALO
    ja, jg = col_a // cw, col_g // cw
    cur = lambda jc: pl.BlockSpec((tm, cw), lambda bi, i: (bi * nt + i, jc))
    prev = lambda jc: pl.BlockSpec(
        (HALO, cw), lambda bi, i: (jnp.maximum((bi * nt + i) * hb - 1, 0), jc))
    nxt = lambda jc: pl.BlockSpec(
        (HALO, cw), lambda bi, i: (jnp.minimum((bi * nt + i + 1) * hb, nrow_h - 1), jc))
    vec = lambda n: pl.BlockSpec((1, n), lambda bi, i: (0, 0))
    return pl.pallas_call(
        functools.partial(_conv_kernel, tm=tm, kw=kw, rc=64),
        grid=(b, nt),
        in_specs=[cur(ja), cur(jg), prev(ja), prev(jg), nxt(ja), nxt(jg),
                  pl.BlockSpec((kw, cw), lambda bi, i: (0, 0)), vec(cw), vec(cw), vec(cw)],
        out_specs=pl.BlockSpec((tm, cw), lambda bi, i: (bi * nt + i, 0)),
        out_shape=jax.ShapeDtypeStruct((b * t, cw), BF16),
        scratch_shapes=[pltpu.VMEM((tm + 2 * HALO, cw), F32), pltpu.VMEM((tm, cw), F32)],
        compiler_params=_params("arbitrary", "arbitrary"),
        name="conformer_conv",
    )(p, p, p, p, p, p, dw_w, dw_b.reshape(1, cw), ln_g.reshape(1, cw), ln_b.reshape(1, cw))


def _out_proj_kernel(att_ref, conv_ref, wa_ref, wc_ref, x_ref, gate_ref, o_ref):
    mix = (jnp.dot(att_ref[...], wa_ref[...], preferred_element_type=F32)
           + jnp.dot(conv_ref[...], wc_ref[...], preferred_element_type=F32))
    o_ref[...] = x_ref[...] + gate_ref[0] * mix


def _out_proj(att, conv, w_out, x2d, gate, b, t):
    m, ka = att.shape
    d = w_out.shape[1]
    tm = _pick(t, (1024, 512, 256, 128))
    tn = _pick(d, (1024, 512, 256, 128))
    nt = t // tm
    return pl.pallas_call(
        _out_proj_kernel,
        grid=(m // tm, d // tn),
        in_specs=[pl.BlockSpec((tm, ka), lambda i, j: (i, 0)),
                  pl.BlockSpec((tm, ka), lambda i, j: (i, 0)),
                  pl.BlockSpec((ka, tn), lambda i, j: (0, j)),
                  pl.BlockSpec((ka, tn), lambda i, j: (1, j)),
                  pl.BlockSpec((tm, tn), lambda i, j: (i, j)),
                  pl.BlockSpec((1, 1, tn), lambda i, j: (i // nt, 0, j))],
        out_specs=pl.BlockSpec((tm, tn), lambda i, j: (i, j)),
        out_shape=jax.ShapeDtypeStruct((m, d), F32),
        compiler_params=_params("arbitrary", "arbitrary"),
        name="out_proj",
    )(att, conv, w_out, w_out, x2d, gate)


def _ffn_up_kernel(h_ref, hp_ref, hn_ref, wu_ref, wg_ref, cwu_ref, cwg_ref, cbu_ref, cbg_ref,
                   o_ref, a_ref, *, tm, nt, tc):
    i = pl.program_id(0)

    @pl.when(pl.program_id(1) == 0)
    def _():
        first = (i % nt) == 0
        last = (i % nt) == nt - 1
        a_ref[0:HALO] = jnp.where(first, jnp.zeros_like(hp_ref), hp_ref[...])
        a_ref[HALO:HALO + tm] = h_ref[...]
        a_ref[HALO + tm:] = jnp.where(last, jnp.zeros_like(hn_ref), hn_ref[...])

    ext = tm + 2 * HALO

    def conv3(w_ref, cw_ref, cb_ref, cols):
        up = jnp.dot(a_ref[...], w_ref[:, cols], preferred_element_type=F32)
        dn = pltpu.roll(up, 1, 0)[HALO:HALO + tm]
        nx = pltpu.roll(up, ext - 1, 0)[HALO:HALO + tm]
        return (dn * cw_ref[0:1, cols] + up[HALO:HALO + tm] * cw_ref[1:2, cols]
                + nx * cw_ref[2:3, cols] + cb_ref[:, cols])

    for c in range(o_ref.shape[1] // tc):
        cols = slice(c * tc, (c + 1) * tc)
        u = conv3(wu_ref, cwu_ref, cbu_ref, cols)
        g = conv3(wg_ref, cwg_ref, cbg_ref, cols)
        o_ref[:, cols] = (g * jax.nn.sigmoid(g) * u).astype(o_ref.dtype)


def _ffn_up(h2, w_up, dw_w, dw_b, b, t):
    m, d = h2.shape
    f = w_up.shape[1] // 2
    tm = _pick(t, (1024, 512, 256, 128))
    tn = _pick(f, (512, 256, 128))
    tc = _pick(tn, (256, 128))
    nt = t // tm
    hb = tm // HALO
    nrow_h = m // HALO
    nf = f // tn
    kw = dw_w.shape[0]
    return pl.pallas_call(
        functools.partial(_ffn_up_kernel, tm=tm, nt=nt, tc=tc),
        grid=(m // tm, nf),
        in_specs=[pl.BlockSpec((tm, d), lambda i, j: (i, 0)),
                  pl.BlockSpec((HALO, d), lambda i, j: (jnp.maximum(i * hb - 1, 0), 0)),
                  pl.BlockSpec((HALO, d), lambda i, j: (jnp.minimum((i + 1) * hb, nrow_h - 1), 0)),
                  pl.BlockSpec((d, tn), lambda i, j: (0, j)),
                  pl.BlockSpec((d, tn), lambda i, j: (0, nf + j)),
                  pl.BlockSpec((kw, tn), lambda i, j: (0, j)),
                  pl.BlockSpec((kw, tn), lambda i, j: (0, nf + j)),
                  pl.BlockSpec((1, tn), lambda i, j: (0, j)),
                  pl.BlockSpec((1, tn), lambda i, j: (0, nf + j))],
        out_specs=pl.BlockSpec((tm, tn), lambda i, j: (i, j)),
        out_shape=jax.ShapeDtypeStruct((m, f), BF16),
        scratch_shapes=[pltpu.VMEM((tm + 2 * HALO, d), BF16)],
        compiler_params=_params("arbitrary", "arbitrary"),
        name="ffn_up",
    )(h2, h2, h2, w_up, w_up, dw_w, dw_w, dw_b.reshape(1, -1), dw_b.reshape(1, -1))


def _pad_ffn(w_up, dw_w, dw_b, w_down):
    f = w_down.shape[0]
    fp = -(-f // FFN_ALIGN) * FFN_ALIGN
    pad2 = lambda a: jnp.pad(a.reshape(a.shape[0], 2, f),
                             ((0, 0), (0, 0), (0, fp - f))).reshape(a.shape[0], 2 * fp)
    return (pad2(w_up.astype(BF16)), pad2(dw_w), pad2(dw_b.reshape(1, -1)).reshape(-1),
            jnp.pad(w_down.astype(BF16), ((0, fp - f), (0, 0))))


def _ffn_down_kernel(a_ref, w_ref, x_ref, gate_ref, o_ref):
    y = jnp.dot(a_ref[...], w_ref[...], preferred_element_type=F32)
    o_ref[...] = x_ref[...] + gate_ref[0] * y


def _ffn_down(act, w_down, x1, gate, t):
    m, f = act.shape
    d = w_down.shape[1]
    tm = _pick(t, (512, 256, 128))
    tn = _pick(d, (512, 256, 128))
    nt = t // tm
    return pl.pallas_call(
        _ffn_down_kernel,
        grid=(m // tm, d // tn),
        in_specs=[pl.BlockSpec((tm, f), lambda i, j: (i, 0)),
                  pl.BlockSpec((f, tn), lambda i, j: (0, j)),
                  pl.BlockSpec((tm, tn), lambda i, j: (i, j)),
                  pl.BlockSpec((1, 1, tn), lambda i, j: (i // nt, 0, j))],
        out_specs=pl.BlockSpec((tm, tn), lambda i, j: (i, j)),
        out_shape=jax.ShapeDtypeStruct((m, d), F32),
        compiler_params=_params("arbitrary", "arbitrary"),
        name="ffn_down",
    )(act, w_down, x1, gate)


def _layer(x, ctx, mods, lam_init, norm1_g, norm2_g, w_in, q_norm_g, k_norm_g, lq1, lk1, lq2, lk2,
           subln_g, conv_dw_w, conv_dw_b, conv_ln_g, conv_ln_b, w_out, w_up, ffn_dw_w, ffn_dw_b,
           w_down):
    b, t, d = x.shape
    n_ctx = ctx.shape[1]
    mix_w = w_out.shape[0]
    att_w = mix_w // 2
    conv_w = mix_w - att_w
    n_heads = att_w // V_HEAD_DIM
    qk_w = n_heads * 2 * HEAD_DIM
    v_w = n_heads * V_HEAD_DIM

    mod_x = mods[:b].reshape(b, N_MOD, 1, d)
    mod_c = jnp.broadcast_to(mods[b].reshape(1, N_MOD, 1, d), (b, N_MOD, 1, d))

    w_in_b = w_in.astype(BF16)
    h_x = _norm_mod(x, norm1_g, mod_x[:, 0], mod_x[:, 1]).reshape(b * t, d)
    h_c = _norm_mod(ctx, norm1_g, mod_c[:, 0], mod_c[:, 1]).reshape(b * n_ctx, d)

    p_x = _matmul(h_x, w_in_b, F32)
    p_c = _matmul(h_c, w_in_b, F32, col0=qk_w, ncols=qk_w + v_w)

    tk = _kv_block(t, n_ctx)
    qt, k_x, vt_x = _qkv_post(p_x, b, t, n_heads, q_norm_g, k_norm_g, tk)
    k_c, vt_c = _kv_post(p_c, b, n_ctx, n_heads, k_norm_g, tk)
    k = jnp.concatenate([k_c, k_x], axis=2)
    vt = jnp.concatenate([vt_c, vt_x], axis=2)
    att = _attention(qt, k, vt, lq1, lk1, lq2, lk2, subln_g, lam_init).reshape(b * t, v_w)

    conv = _conformer_conv(p_x, b, t, 2 * qk_w + v_w, 2 * qk_w + v_w + conv_w, conv_w,
                           conv_dw_w, conv_dw_b, conv_ln_g, conv_ln_b)

    x1 = _out_proj(att, conv, w_out.astype(BF16), x.reshape(b * t, d), mod_x[:, 2], b, t)
    h2 = _norm_mod(x1.reshape(b, t, d), norm2_g, mod_x[:, 3], mod_x[:, 4]).reshape(b * t, d)
    w_up_p, dw_w_p, dw_b_p, w_down_p = _pad_ffn(w_up, ffn_dw_w, ffn_dw_b, w_down)
    act = _ffn_up(h2, w_up_p, dw_w_p, dw_b_p, b, t)
    out = _ffn_down(act, w_down_p, x1, mod_x[:, 5], t)
    return out.reshape(b, t, d)


def kernel(x, c, ctx, c_ctx, w_ada, b_ada, norm1_g, norm2_g, w_in, q_norm_g, k_norm_g, lambda_q1,
           lambda_k1, lambda_q2, lambda_k2, subln_g, conv_dw_w, conv_dw_b, conv_ln_g, conv_ln_b,
           w_out, w_up, ffn_dw_w, ffn_dw_b, w_down):
    b, t, d = x.shape
    depth = w_ada.shape[0]
    assert depth == 1, "context-stream updates are only needed for depth > 1"
    i = 0
    lam_init = 0.8 - 0.6 * math.exp(-0.3 * i)
    cc = jnp.zeros((8, d), F32).at[:b].set(c).at[b].set(c_ctx)
    mods = _ada(cc, w_ada[i], b_ada[i])
    return _layer(x, ctx, mods, lam_init, norm1_g[i], norm2_g[i], w_in[i], q_norm_g[i],
                  k_norm_g[i], lambda_q1[i], lambda_k1[i], lambda_q2[i], lambda_k2[i], subln_g[i],
                  conv_dw_w[i], conv_dw_b[i], conv_ln_g[i], conv_ln_b[i], w_out[i], w_up[i],
                  ffn_dw_w[i], ffn_dw_b[i], w_down[i])
```

```python
import functools
import math

import jax
import jax.numpy as jnp
from jax import lax
from jax.experimental import pallas as pl
from jax.experimental.pallas import tpu as pltpu

F32 = jnp.float32
BF16 = jnp.bfloat16

HEAD_DIM = 64
V_HEAD_DIM = 2 * HEAD_DIM
GRID_W = 64
ROPE_BASE = 10000.0
N_MOD = 6
LOG2E = math.log2(math.e)
ATTN_MAX_TK = 2816
LANES = 128
HALO = 16
VMEM_LIMIT = 56 * 1024 * 1024


def _params(*sem, flags=None):
    return pltpu.CompilerParams(dimension_semantics=sem, vmem_limit_bytes=VMEM_LIMIT, flags=flags)


def _pick(n, prefs):
    for p in prefs:
        if n % p == 0:
            return p
    return n


def _ada_kernel(c_ref, w_ref, b_ref, o_ref):
    c = c_ref[...]
    s = (c * jax.nn.sigmoid(c)).astype(BF16)
    o_ref[...] = jnp.dot(s, w_ref[...].astype(BF16), preferred_element_type=F32) + b_ref[...]


def _ada(cc, w_ada, b_ada):
    d, n = w_ada.shape
    tn = _pick(n, (512, 256, 128))
    return pl.pallas_call(
        _ada_kernel,
        grid=(n // tn,),
        in_specs=[pl.BlockSpec((8, d), lambda j: (0, 0)),
                  pl.BlockSpec((d, tn), lambda j: (0, j)),
                  pl.BlockSpec((1, tn), lambda j: (0, j))],
        out_specs=pl.BlockSpec((8, tn), lambda j: (0, j)),
        out_shape=jax.ShapeDtypeStruct((8, n), F32),
        compiler_params=_params("arbitrary"),
        name="ada",
    )(cc, w_ada, b_ada.reshape(1, n))


def _norm_mod_kernel(x_ref, g_ref, shift_ref, scale_ref, o_ref):
    x = x_ref[0]
    ms = jnp.mean(x * x, axis=-1, keepdims=True)
    y = x * lax.rsqrt(ms + 1e-6) * g_ref[...]
    o_ref[0] = (y * (1.0 + scale_ref[0]) + shift_ref[0]).astype(o_ref.dtype)


def _norm_mod(x, g, shift, scale):
    b, t, d = x.shape
    tm = _pick(t, (512, 256, 128))
    return pl.pallas_call(
        _norm_mod_kernel,
        grid=(b, t // tm),
        in_specs=[pl.BlockSpec((1, tm, d), lambda bi, i: (bi, i, 0)),
                  pl.BlockSpec((1, d), lambda bi, i: (0, 0)),
                  pl.BlockSpec((1, 1, d), lambda bi, i: (bi, 0, 0)),
                  pl.BlockSpec((1, 1, d), lambda bi, i: (bi, 0, 0))],
        out_specs=pl.BlockSpec((1, tm, d), lambda bi, i: (bi, i, 0)),
        out_shape=jax.ShapeDtypeStruct((b, t, d), BF16),
        compiler_params=_params("arbitrary", "arbitrary"),
        name="norm_mod",
    )(x, g.reshape(1, d), shift, scale)


def _mm_kernel(a_ref, b_ref, o_ref):
    o_ref[...] = jnp.dot(a_ref[...], b_ref[...], preferred_element_type=F32).astype(o_ref.dtype)


def _matmul(a, w, out_dtype, col0=0, ncols=None):
    m, k = a.shape
    n = w.shape[1] if ncols is None else ncols
    tm = _pick(m, (1024, 512, 256, 128))
    tn = _pick(math.gcd(n, col0) if col0 else n, (1024, 512, 256, 128))
    c0 = col0 // tn
    return pl.pallas_call(
        _mm_kernel,
        grid=(m // tm, n // tn),
        in_specs=[pl.BlockSpec((tm, k), lambda i, j: (i, 0)),
                  pl.BlockSpec((k, tn), lambda i, j: (0, c0 + j))],
        out_specs=pl.BlockSpec((tm, tn), lambda i, j: (i, j)),
        out_shape=jax.ShapeDtypeStruct((m, n), out_dtype),
        compiler_params=_params("arbitrary", "arbitrary"),
        name="matmul",
    )(a, w)


def _subhead_rms(x, g, ones_bd):
    sq = x * x
    hi = sq.astype(BF16)
    lo = (sq - hi.astype(F32)).astype(BF16)
    ss = (jnp.dot(hi, ones_bd, preferred_element_type=F32)
          + jnp.dot(lo, ones_bd, preferred_element_type=F32))
    return x * lax.rsqrt(ss * (1.0 / HEAD_DIM) + 1e-6) * g


def _rope(x, cos, sin_signed, even):
    w = x.shape[-1]
    nxt = pltpu.roll(x, w - 1, 1)
    prv = pltpu.roll(x, 1, 1)
    return x * cos + jnp.where(even, nxt, prv) * sin_signed


def _qkv_post_kernel(q_ref, k_ref, v_ref, cos_ref, sin_ref, gq_ref, gk_ref, ones_ref,
                     qo_ref, ko_ref, vo_ref, *, nh, tk):
    ones_bd = ones_ref[...]
    cos = cos_ref[...]
    sin = sin_ref[...]
    tm = cos.shape[0]
    lane = lax.broadcasted_iota(jnp.int32, cos.shape, 1)
    even = (lane % 2) == 0
    sub1 = lax.broadcasted_iota(jnp.int32, (V_HEAD_DIM, tm), 0) < HEAD_DIM
    for h in range(nh):
        sl = slice(h * V_HEAD_DIM, (h + 1) * V_HEAD_DIM)
        q = _subhead_rms(q_ref[:, sl], gq_ref[...], ones_bd)
        qt = (_rope(q, cos, sin, even) * (HEAD_DIM ** -0.5 * LOG2E)).T
        qo_ref[0, h, 0, 0] = jnp.where(sub1, qt, 0.0).astype(BF16)
        qo_ref[0, h, 0, 1] = jnp.where(sub1, 0.0, qt).astype(BF16)
        k = _subhead_rms(k_ref[:, sl], gk_ref[...], ones_bd)
        ko_ref[0, h] = _rope(k, cos, sin, even).astype(BF16)
        for c in range(tm // tk):
            vo_ref[0, h, c] = v_ref[c * tk:(c + 1) * tk, sl].T.astype(BF16)


def _kv_post_kernel(k_ref, v_ref, gk_ref, ones_ref, ko_ref, vo_ref, *, nh, tk):
    ones_bd = ones_ref[...]
    tm = k_ref.shape[0]
    for h in range(nh):
        sl = slice(h * V_HEAD_DIM, (h + 1) * V_HEAD_DIM)
        ko_ref[0, h] = _subhead_rms(k_ref[:, sl], gk_ref[...], ones_bd).astype(BF16)
        for c in range(tm // tk):
            vo_ref[0, h, c] = v_ref[c * tk:(c + 1) * tk, sl].T.astype(BF16)


def _ones_blockdiag():
    r = jnp.arange(LANES) // HEAD_DIM
    return (r[:, None] == r[None, :]).astype(BF16)


def _rope_tables(t):
    rows = t // GRID_W
    row = jnp.broadcast_to(jnp.arange(rows)[:, None], (rows, GRID_W)).reshape(-1)
    col = jnp.broadcast_to(jnp.arange(GRID_W)[None, :], (rows, GRID_W)).reshape(-1)
    n_pairs_axis = HEAD_DIM // 4
    freqs = ROPE_BASE ** (-jnp.arange(n_pairs_axis, dtype=F32) / n_pairs_axis)
    ang = jnp.concatenate([row.astype(F32)[:, None] * freqs,
                           col.astype(F32)[:, None] * freqs], axis=-1)
    cos = jnp.repeat(jnp.cos(ang), 2, axis=-1)
    sin = jnp.repeat(jnp.sin(ang), 2, axis=-1)
    sign = jnp.where(jnp.arange(HEAD_DIM) % 2 == 0, -1.0, 1.0).astype(F32)
    reps = LANES // HEAD_DIM
    return jnp.tile(cos, (1, reps)), jnp.tile(sin * sign, (1, reps))


def _kv_block(t, n_ctx):
    return _pick(math.gcd(t, n_ctx), (256, 128))


def _qkv_post(p, b, t, n_heads, gq, gk, tk):
    qk_w = n_heads * V_HEAD_DIM
    nh = _pick(n_heads, (4, 2, 1))
    wblk = nh * V_HEAD_DIM
    tm = _pick(t, (512, 256, 128))
    nt = t // tm
    nq = qk_w // wblk
    nc = tm // tk
    cos, sin = _rope_tables(t)
    g2 = lambda g: jnp.tile(g, LANES // HEAD_DIM).reshape(1, LANES)
    row = lambda bi, i, j: bi * nt + i
    return pl.pallas_call(
        functools.partial(_qkv_post_kernel, nh=nh, tk=tk),
        grid=(b, nt, nq),
        in_specs=[pl.BlockSpec((tm, wblk), lambda bi, i, j: (row(bi, i, j), j)),
                  pl.BlockSpec((tm, wblk), lambda bi, i, j: (row(bi, i, j), nq + j)),
                  pl.BlockSpec((tm, wblk), lambda bi, i, j: (row(bi, i, j), 2 * nq + j)),
                  pl.BlockSpec((tm, LANES), lambda bi, i, j: (i, 0)),
                  pl.BlockSpec((tm, LANES), lambda bi, i, j: (i, 0)),
                  pl.BlockSpec((1, LANES), lambda bi, i, j: (0, 0)),
                  pl.BlockSpec((1, LANES), lambda bi, i, j: (0, 0)),
                  pl.BlockSpec((LANES, LANES), lambda bi, i, j: (0, 0))],
        out_specs=[pl.BlockSpec((1, nh, 1, 2, V_HEAD_DIM, tm), lambda bi, i, j: (bi, j, i, 0, 0, 0)),
                   pl.BlockSpec((1, nh, tm, LANES), lambda bi, i, j: (bi, j, i, 0)),
                   pl.BlockSpec((1, nh, nc, V_HEAD_DIM, tk), lambda bi, i, j: (bi, j, i, 0, 0))],
        out_shape=[jax.ShapeDtypeStruct((b, n_heads, nt, 2, V_HEAD_DIM, tm), BF16),
                   jax.ShapeDtypeStruct((b, n_heads, t, LANES), BF16),
                   jax.ShapeDtypeStruct((b, n_heads, t // tk, V_HEAD_DIM, tk), BF16)],
        compiler_params=_params("arbitrary", "arbitrary", "arbitrary"),
        name="qkv_post",
    )(p, p, p, cos, sin, g2(gq), g2(gk), _ones_blockdiag())


def _kv_post(pc, b, t, n_heads, gk, tk):
    qk_w = n_heads * V_HEAD_DIM
    nh = _pick(n_heads, (4, 2, 1))
    wblk = nh * V_HEAD_DIM
    tm = _pick(t, (256, 128))
    nt = t // tm
    nq = qk_w // wblk
    nc = tm // tk
    g2 = jnp.tile(gk, LANES // HEAD_DIM).reshape(1, LANES)
    return pl.pallas_call(
        functools.partial(_kv_post_kernel, nh=nh, tk=tk),
        grid=(b, nt, nq),
        in_specs=[pl.BlockSpec((tm, wblk), lambda bi, i, j: (bi * nt + i, j)),
                  pl.BlockSpec((tm, wblk), lambda bi, i, j: (bi * nt + i, nq + j)),
                  pl.BlockSpec((1, LANES), lambda bi, i, j: (0, 0)),
                  pl.BlockSpec((LANES, LANES), lambda bi, i, j: (0, 0))],
        out_specs=[pl.BlockSpec((1, nh, tm, LANES), lambda bi, i, j: (bi, j, i, 0)),
                   pl.BlockSpec((1, nh, nc, V_HEAD_DIM, tk), lambda bi, i, j: (bi, j, i, 0, 0))],
        out_shape=[jax.ShapeDtypeStruct((b, n_heads, t, LANES), BF16),
                   jax.ShapeDtypeStruct((b, n_heads, t // tk, V_HEAD_DIM, tk), BF16)],
        compiler_params=_params("arbitrary", "arbitrary", "arbitrary"),
        name="kv_post",
    )(pc, pc, g2, _ones_blockdiag())


def _attn_kernel(qt_ref, k_ref, vt_ref, lq1_ref, lk1_ref, lq2_ref, lk2_ref, g_ref, o_ref,
                 q2_ref, s_ref, p_ref, mx_ref, a_ref, m_ref, l_ref, acc_ref,
                 *, tq, tkw, sub, nstep, ntile, lam_init):
    assert nstep >= 2
    tk = tkw * sub
    nw = 2 * tq
    cw = _pick(nw, (2 * LANES, LANES))
    chunks = [slice(c * cw, (c + 1) * cw) for c in range(nw // cw)]
    acc_ref[...] = jnp.zeros_like(acc_ref)
    l_ref[...] = jnp.zeros_like(l_ref)
    m_ref[...] = jnp.zeros_like(m_ref)

    def load_q(i):
        q2_ref[:, :tq] = qt_ref[0, 0, i, 0]
        q2_ref[:, tq:] = qt_ref[0, 0, i, 1]

    def stage_p(t):
        for cols in chunks:
            pv = None
            for j in range(sub):
                d = jnp.dot(vt_ref[0, 0, t * sub + j], p_ref[j * tkw:(j + 1) * tkw, cols],
                            preferred_element_type=F32)
                pv = d if pv is None else pv + d
            acc_ref[:, cols] = a_ref[:, cols] * acc_ref[:, cols] + pv

    def stage_s(first, par):
        m_prev = jnp.where(first, -jnp.inf, m_ref[...])
        m_new = jnp.maximum(m_prev, mx_ref[...])
        alpha = jnp.exp2(m_prev - m_new)
        for cols in chunks:
            p = jnp.exp2(s_ref[:, cols] - m_new[:, cols])
            l_ref[par, :, cols] = (alpha[:, cols] * l_ref[par, :, cols]
                                   + jnp.sum(p, axis=0, keepdims=True))
            p_ref[:, cols] = p.astype(BF16)
        m_ref[...] = m_new
        a_ref[...] = alpha

    def stage_q(t):
        kb = k_ref[0, 0, pl.ds(pl.multiple_of(t * tk, tk), tk), :]
        for cols in chunks:
            s = jnp.dot(kb, q2_ref[:, cols], preferred_element_type=F32)
            s_ref[:, cols] = s
            mx_ref[:, cols] = jnp.max(s, axis=0, keepdims=True)

    def finalize(i):
        lam = (jnp.exp(jnp.sum(lq1_ref[...] * lk1_ref[...], axis=-1, keepdims=True))
               - jnp.exp(jnp.sum(lq2_ref[...] * lk2_ref[...], axis=-1, keepdims=True)) + lam_init)
        o12 = acc_ref[...] * (1.0 / l_ref[i % 2])
        o = o12[:, :tq] - lam * o12[:, tq:]
        ms = jnp.mean(o * o, axis=0, keepdims=True)
        y = o * lax.rsqrt(ms + 1e-6) * g_ref[...] * (1.0 - lam_init)
        o_ref[0, pl.ds(pl.multiple_of(i * tq, tq), tq), :] = y.T.astype(o_ref.dtype)

    load_q(0)
    stage_q(0)
    stage_s(True, 0)
    stage_q(1)

    def step(g, carry):
        iq = g // nstep
        tq_ = g - iq * nstep
        ts = (g - 1) % nstep
        ip = (g - 2) // nstep
        tp = (g - 2) - ip * nstep

        @pl.when(tq_ == 0)
        def _():
            load_q(iq)

        stage_p(tp)
        stage_s(ts == 0, ((g - 1) // nstep) % 2)
        stage_q(tq_)

        @pl.when(tp == nstep - 1)
        def _():
            finalize(ip)

        return carry

    lax.fori_loop(2, ntile * nstep, step, 0)

    stage_p(nstep - 2)
    stage_s(False, (ntile - 1) % 2)
    stage_p(nstep - 1)
    finalize(ntile - 1)


def _attention(qt, k, vt, lq1, lk1, lq2, lk2, subln_g, lam_init):
    b, h, ntile, _, _, tq = qt.shape
    t = ntile * tq
    l = k.shape[2]
    nblk, _, tkw = vt.shape[2:]
    sub = max(d for d in range(1, nblk) if nblk % d == 0 and d * tkw <= ATTN_MAX_TK)
    tk = sub * tkw
    vec = lambda a: a.reshape(1, -1)
    small = lambda n: pl.BlockSpec((1, n), lambda bi, hi: (0, 0))
    stat = pltpu.VMEM((1, 2 * tq), F32)
    return pl.pallas_call(
        functools.partial(_attn_kernel, tq=tq, tkw=tkw, sub=sub, nstep=nblk // sub, ntile=ntile,
                          lam_init=lam_init),
        grid=(b, h),
        in_specs=[pl.BlockSpec((1, 1, ntile, 2, V_HEAD_DIM, tq), lambda bi, hi: (bi, hi, 0, 0, 0, 0)),
                  pl.BlockSpec((1, 1, l, V_HEAD_DIM), lambda bi, hi: (bi, hi, 0, 0)),
                  pl.BlockSpec((1, 1, nblk, V_HEAD_DIM, tkw), lambda bi, hi: (bi, hi, 0, 0, 0)),
                  small(HEAD_DIM), small(HEAD_DIM), small(HEAD_DIM), small(HEAD_DIM),
                  pl.BlockSpec((V_HEAD_DIM, 1), lambda bi, hi: (0, 0))],
        out_specs=pl.BlockSpec((1, t, V_HEAD_DIM), lambda bi, hi: (bi, 0, hi)),
        out_shape=jax.ShapeDtypeStruct((b, t, h * V_HEAD_DIM), BF16),
        scratch_shapes=[pltpu.VMEM((V_HEAD_DIM, 2 * tq), BF16),
                        pltpu.VMEM((tk, 2 * tq), F32), pltpu.VMEM((tk, 2 * tq), BF16),
                        stat, stat, stat, pltpu.VMEM((2, 1, 2 * tq), F32),
                        pltpu.VMEM((V_HEAD_DIM, 2 * tq), F32)],
        compiler_params=_params("arbitrary", "arbitrary"),
        name="attention",
    )(qt, k, vt, vec(lq1), vec(lk1), vec(lq2), vec(lk2), subln_g.reshape(-1, 1))


def _conv_kernel(ca_ref, cg_ref, cap_ref, cgp_ref, can_ref, cgn_ref, w_ref, b_ref, lg_ref, lb_ref,
                 o_ref, u_ref, y_ref, *, tm, kw, rc):
    i = pl.program_id(1)
    pad = (kw - 1) // 2
    glu = lambda a, g: a * jax.nn.sigmoid(g)
    first = i == 0
    last = i == pl.num_programs(1) - 1
    u_ref[0:HALO] = jnp.where(first, 0.0, glu(cap_ref[...], cgp_ref[...]))
    u_ref[HALO:HALO + tm] = glu(ca_ref[...], cg_ref[...])
    u_ref[HALO + tm:] = jnp.where(last, 0.0, glu(can_ref[...], cgn_ref[...]))
    nstrip = u_ref.shape[1] // LANES

    def strip(c, carry):
        lanes = pl.ds(pl.multiple_of(c * LANES, LANES), LANES)
        for r in range(tm // rc):
            span = HALO - pad + kw - 1
            nwin = rc + 8 * (span // 8 + 1)
            win = u_ref[r * rc:r * rc + nwin, lanes]
            acc = jnp.zeros((rc, LANES), F32)
            for rho in range(8):
                offs = [o for o in range(HALO - pad, span + 1) if o % 8 == rho]
                if not offs:
                    continue
                sh = pltpu.roll(win, nwin - rho, 0) if rho else win
                for o in offs:
                    k = o - (HALO - pad)
                    q8 = o - rho
                    acc = acc + sh[q8:q8 + rc] * w_ref[k:k + 1, lanes]
            y_ref[r * rc:(r + 1) * rc, lanes] = acc + b_ref[:, lanes]
        return carry

    lax.fori_loop(0, nstrip, strip, 0)
    y = y_ref[...]
    mu = jnp.mean(y, axis=-1, keepdims=True)
    yc = y - mu
    var = jnp.mean(yc * yc, axis=-1, keepdims=True)
    z = yc * lax.rsqrt(var + 1e-5) * lg_ref[...] + lb_ref[...]
    o_ref[...] = (z * jax.nn.sigmoid(z)).astype(o_ref.dtype)


def _conformer_conv(p, b, t, col_a, col_g, cw, dw_w, dw_b, ln_g, ln_b):
    kw = dw_w.shape[0]
    tm = _pick(t, (256, 128))
    nt = t // tm
    hb = tm // HALO
    nrow_h = (b * t) // HALO
    ja, jg = col_a // cw, col_g // cw
    cur = lambda jc: pl.BlockSpec((tm, cw), lambda bi, i: (bi * nt + i, jc))
    prev = lambda jc: pl.BlockSpec(
        (HALO, cw), lambda bi, i: (jnp.maximum((bi * nt + i) * hb - 1, 0), jc))
    nxt = lambda jc: pl.BlockSpec(
        (HALO, cw), lambda bi, i: (jnp.minimum((bi * nt + i + 1) * hb, nrow_h - 1), jc))
    vec = lambda n: pl.BlockSpec((1, n), lambda bi, i: (0, 0))
    return pl.pallas_call(
        functools.partial(_conv_kernel, tm=tm, kw=kw, rc=64),
        grid=(b, nt),
        in_specs=[cur(ja), cur(jg), prev(ja), prev(jg), nxt(ja), nxt(jg),
                  pl.BlockSpec((kw, cw), lambda bi, i: (0, 0)), vec(cw), vec(cw), vec(cw)],
        out_specs=pl.BlockSpec((tm, cw), lambda bi, i: (bi * nt + i, 0)),
        out_shape=jax.ShapeDtypeStruct((b * t, cw), BF16),
        scratch_shapes=[pltpu.VMEM((tm + 2 * HALO, cw), F32), pltpu.VMEM((tm, cw), F32)],
        compiler_params=_params("arbitrary", "arbitrary"),
        name="conformer_conv",
    )(p, p, p, p, p, p, dw_w, dw_b.reshape(1, cw), ln_g.reshape(1, cw), ln_b.reshape(1, cw))


def _out_proj_kernel(att_ref, conv_ref, wa_ref, wc_ref, x_ref, gate_ref, o_ref):
    mix = (jnp.dot(att_ref[...], wa_ref[...], preferred_element_type=F32)
           + jnp.dot(conv_ref[...], wc_ref[...], preferred_element_type=F32))
    o_ref[...] = x_ref[...] + gate_ref[0] * mix


def _out_proj(att, conv, w_out, x2d, gate, b, t):
    m, ka = att.shape
    d = w_out.shape[1]
    tm = _pick(t, (1024, 512, 256, 128))
    tn = _pick(d, (1024, 512, 256, 128))
    nt = t // tm
    return pl.pallas_call(
        _out_proj_kernel,
        grid=(m // tm, d // tn),
        in_specs=[pl.BlockSpec((tm, ka), lambda i, j: (i, 0)),
                  pl.BlockSpec((tm, ka), lambda i, j: (i, 0)),
                  pl.BlockSpec((ka, tn), lambda i, j: (0, j)),
                  pl.BlockSpec((ka, tn), lambda i, j: (1, j)),
                  pl.BlockSpec((tm, tn), lambda i, j: (i, j)),
                  pl.BlockSpec((1, 1, tn), lambda i, j: (i // nt, 0, j))],
        out_specs=pl.BlockSpec((tm, tn), lambda i, j: (i, j)),
        out_shape=jax.ShapeDtypeStruct((m, d), F32),
        compiler_params=_params("arbitrary", "arbitrary"),
        name="out_proj",
    )(att, conv, w_out, w_out, x2d, gate)


def _ffn_up_kernel(h_ref, hp_ref, hn_ref, *refs, tm, nt, nchunk):
    o_ref, a_ref = refs[6 * nchunk:]
    i = pl.program_id(0)

    @pl.when(pl.program_id(1) == 0)
    def _():
        first = (i % nt) == 0
        last = (i % nt) == nt - 1
        a_ref[0:HALO] = jnp.where(first, jnp.zeros_like(hp_ref), hp_ref[...])
        a_ref[HALO:HALO + tm] = h_ref[...]
        a_ref[HALO + tm:] = jnp.where(last, jnp.zeros_like(hn_ref), hn_ref[...])

    ext = tm + 2 * HALO

    def conv3(w_ref, cw_ref, cb_ref):
        up = jnp.dot(a_ref[...], w_ref[...], preferred_element_type=F32)
        dn = pltpu.roll(up, 1, 0)[HALO:HALO + tm]
        nx = pltpu.roll(up, ext - 1, 0)[HALO:HALO + tm]
        return (dn * cw_ref[0:1] + up[HALO:HALO + tm] * cw_ref[1:2] + nx * cw_ref[2:3]
                + cb_ref[...])

    for c in range(nchunk):
        wu_ref, wg_ref, cwu_ref, cwg_ref, cbu_ref, cbg_ref = refs[6 * c:6 * c + 6]
        tc = wu_ref.shape[1]
        u = conv3(wu_ref, cwu_ref, cbu_ref)
        g = conv3(wg_ref, cwg_ref, cbg_ref)
        o_ref[:, c * tc:(c + 1) * tc] = (g * jax.nn.sigmoid(g) * u).astype(o_ref.dtype)


def _ffn_up(h2, w_up, dw_w, dw_b, b, t):
    m, d = h2.shape
    f = w_up.shape[1] // 2
    tm = _pick(t, (1024, 512, 256, 128))
    tc = _pick(f, (256, 128))
    nchunk = 2
    tn = nchunk * tc
    nt = t // tm
    hb = tm // HALO
    nrow_h = m // HALO
    nfc = f // tc
    kw = dw_w.shape[0]
    db = dw_b.reshape(1, -1)
    col = lambda half, c: (lambda i, j: (0, jnp.minimum(half * nfc + j * nchunk + c, 2 * nfc - 1)))
    specs, args = [], []
    for c in range(nchunk):
        for rows, arr in ((d, w_up), (kw, dw_w), (1, db)):
            for half in (0, 1):
                specs.append(pl.BlockSpec((rows, tc), col(half, c)))
                args.append(arr)
    return pl.pallas_call(
        functools.partial(_ffn_up_kernel, tm=tm, nt=nt, nchunk=nchunk),
        grid=(m // tm, pl.cdiv(f, tn)),
        in_specs=[pl.BlockSpec((tm, d), lambda i, j: (i, 0)),
                  pl.BlockSpec((HALO, d), lambda i, j: (jnp.maximum(i * hb - 1, 0), 0)),
                  pl.BlockSpec((HALO, d), lambda i, j: (jnp.minimum((i + 1) * hb, nrow_h - 1), 0))]
                 + specs,
        out_specs=pl.BlockSpec((tm, tn), lambda i, j: (i, j)),
        out_shape=jax.ShapeDtypeStruct((m, f), BF16),
        scratch_shapes=[pltpu.VMEM((tm + 2 * HALO, d), BF16)],
        compiler_params=_params("arbitrary", "arbitrary"),
        name="ffn_up",
    )(h2, h2, h2, *args)


def _ffn_down_kernel(a_ref, w_ref, x_ref, gate_ref, o_ref):
    y = jnp.dot(a_ref[...], w_ref[...], preferred_element_type=F32)
    o_ref[...] = x_ref[...] + gate_ref[0] * y


def _ffn_down(act, w_down, x1, gate, t):
    m, f = act.shape
    d = w_down.shape[1]
    tm = _pick(t, (512, 256, 128))
    tn = _pick(d, (512, 256, 128))
    nt = t // tm
    return pl.pallas_call(
        _ffn_down_kernel,
        grid=(m // tm, d // tn),
        in_specs=[pl.BlockSpec((tm, f), lambda i, j: (i, 0)),
                  pl.BlockSpec((f, tn), lambda i, j: (0, j)),
                  pl.BlockSpec((tm, tn), lambda i, j: (i, j)),
                  pl.BlockSpec((1, 1, tn), lambda i, j: (i // nt, 0, j))],
        out_specs=pl.BlockSpec((tm, tn), lambda i, j: (i, j)),
        out_shape=jax.ShapeDtypeStruct((m, d), F32),
        compiler_params=_params("arbitrary", "arbitrary"),
        name="ffn_down",
    )(act, w_down, x1, gate)


def _layer(x, ctx, mods, lam_init, norm1_g, norm2_g, w_in, q_norm_g, k_norm_g, lq1, lk1, lq2, lk2,
           subln_g, conv_dw_w, conv_dw_b, conv_ln_g, conv_ln_b, w_out, w_up, ffn_dw_w, ffn_dw_b,
           w_down):
    b, t, d = x.shape
    n_ctx = ctx.shape[1]
    mix_w = w_out.shape[0]
    att_w = mix_w // 2
    conv_w = mix_w - att_w
    n_heads = att_w // V_HEAD_DIM
    qk_w = n_heads * 2 * HEAD_DIM
    v_w = n_heads * V_HEAD_DIM

    mod_x = mods[:b].reshape(b, N_MOD, 1, d)
    mod_c = jnp.broadcast_to(mods[b].reshape(1, N_MOD, 1, d), (b, N_MOD, 1, d))

    w_in_b = w_in.astype(BF16)
    h_x = _norm_mod(x, norm1_g, mod_x[:, 0], mod_x[:, 1]).reshape(b * t, d)
    h_c = _norm_mod(ctx, norm1_g, mod_c[:, 0], mod_c[:, 1]).reshape(b * n_ctx, d)

    p_x = _matmul(h_x, w_in_b, F32)
    p_c = _matmul(h_c, w_in_b, F32, col0=qk_w, ncols=qk_w + v_w)

    tk = _kv_block(t, n_ctx)
    qt, k_x, vt_x = _qkv_post(p_x, b, t, n_heads, q_norm_g, k_norm_g, tk)
    k_c, vt_c = _kv_post(p_c, b, n_ctx, n_heads, k_norm_g, tk)
    k = jnp.concatenate([k_c, k_x], axis=2)
    vt = jnp.concatenate([vt_c, vt_x], axis=2)
    att = _attention(qt, k, vt, lq1, lk1, lq2, lk2, subln_g, lam_init).reshape(b * t, v_w)

    conv = _conformer_conv(p_x, b, t, 2 * qk_w + v_w, 2 * qk_w + v_w + conv_w, conv_w,
                           conv_dw_w, conv_dw_b, conv_ln_g, conv_ln_b)

    x1 = _out_proj(att, conv, w_out.astype(BF16), x.reshape(b * t, d), mod_x[:, 2], b, t)
    h2 = _norm_mod(x1.reshape(b, t, d), norm2_g, mod_x[:, 3], mod_x[:, 4]).reshape(b * t, d)
    act = _ffn_up(h2, w_up.astype(BF16), ffn_dw_w, ffn_dw_b, b, t)
    out = _ffn_down(act, w_down.astype(BF16), x1, mod_x[:, 5], t)
    return out.reshape(b, t, d)


def kernel(x, c, ctx, c_ctx, w_ada, b_ada, norm1_g, norm2_g, w_in, q_norm_g, k_norm_g, lambda_q1,
           lambda_k1, lambda_q2, lambda_k2, subln_g, conv_dw_w, conv_dw_b, conv_ln_g, conv_ln_b,
           w_out, w_up, ffn_dw_w, ffn_dw_b, w_down):
    b, t, d = x.shape
    depth = w_ada.shape[0]
    assert depth == 1, "context-stream updates are only needed for depth > 1"
    i = 0
    lam_init = 0.8 - 0.6 * math.exp(-0.3 * i)
    cc = jnp.zeros((8, d), F32).at[:b].set(c).at[b].set(c_ctx)
    mods = _ada(cc, w_ada[i], b_ada[i])
    return _layer(x, ctx, mods, lam_init, norm1_g[i], norm2_g[i], w_in[i], q_norm_g[i],
                  k_norm_g[i], lambda_q1[i], lambda_k1[i], lambda_q2[i], lambda_k2[i], subln_g[i],
                  conv_dw_w[i], conv_dw_b[i], conv_ln_g[i], conv_ln_b[i], w_out[i], w_up[i],
                  ffn_dw_w[i], ffn_dw_b[i], w_down[i])
```

```python
import functools
import math

import jax
import jax.numpy as jnp
from jax import lax
from jax.experimental import pallas as pl
from jax.experimental.pallas import tpu as pltpu

F32 = jnp.float32
BF16 = jnp.bfloat16

HEAD_DIM = 64
V_HEAD_DIM = 2 * HEAD_DIM
GRID_W = 64
ROPE_BASE = 10000.0
N_MOD = 6
LOG2E = math.log2(math.e)
ATTN_MAX_TK = 2816
LANES = 128
HALO = 16
VMEM_LIMIT = 56 * 1024 * 1024


def _params(*sem, flags=None):
    return pltpu.CompilerParams(dimension_semantics=sem, vmem_limit_bytes=VMEM_LIMIT, flags=flags)


def _pick(n, prefs):
    for p in prefs:
        if n % p == 0:
            return p
    return n


def _ada_kernel(c_ref, w_ref, b_ref, o_ref):
    c = c_ref[...]
    s = (c * jax.nn.sigmoid(c)).astype(BF16)
    o_ref[...] = jnp.dot(s, w_ref[...].astype(BF16), preferred_element_type=F32) + b_ref[...]


def _ada(cc, w_ada, b_ada):
    d, n = w_ada.shape
    tn = _pick(n, (512, 256, 128))
    return pl.pallas_call(
        _ada_kernel,
        grid=(n // tn,),
        in_specs=[pl.BlockSpec((8, d), lambda j: (0, 0)),
                  pl.BlockSpec((d, tn), lambda j: (0, j)),
                  pl.BlockSpec((1, tn), lambda j: (0, j))],
        out_specs=pl.BlockSpec((8, tn), lambda j: (0, j)),
        out_shape=jax.ShapeDtypeStruct((8, n), F32),
        compiler_params=_params("arbitrary"),
        name="ada",
    )(cc, w_ada, b_ada.reshape(1, n))


def _norm_mod_kernel(x_ref, g_ref, shift_ref, scale_ref, o_ref):
    x = x_ref[0]
    ms = jnp.mean(x * x, axis=-1, keepdims=True)
    y = x * lax.rsqrt(ms + 1e-6) * g_ref[...]
    o_ref[0] = (y * (1.0 + scale_ref[0]) + shift_ref[0]).astype(o_ref.dtype)


def _norm_mod(x, g, shift, scale):
    b, t, d = x.shape
    tm = _pick(t, (512, 256, 128))
    return pl.pallas_call(
        _norm_mod_kernel,
        grid=(b, t // tm),
        in_specs=[pl.BlockSpec((1, tm, d), lambda bi, i: (bi, i, 0)),
                  pl.BlockSpec((1, d), lambda bi, i: (0, 0)),
                  pl.BlockSpec((1, 1, d), lambda bi, i: (bi, 0, 0)),
                  pl.BlockSpec((1, 1, d), lambda bi, i: (bi, 0, 0))],
        out_specs=pl.BlockSpec((1, tm, d), lambda bi, i: (bi, i, 0)),
        out_shape=jax.ShapeDtypeStruct((b, t, d), BF16),
        compiler_params=_params("arbitrary", "arbitrary"),
        name="norm_mod",
    )(x, g.reshape(1, d), shift, scale)


def _mm_kernel(a_ref, b_ref, o_ref):
    o_ref[...] = jnp.dot(a_ref[...], b_ref[...], preferred_element_type=F32).astype(o_ref.dtype)


def _matmul(a, w, out_dtype, col0=0, ncols=None):
    m, k = a.shape
    n = w.shape[1] if ncols is None else ncols
    tm = _pick(m, (1024, 512, 256, 128))
    tn = _pick(math.gcd(n, col0) if col0 else n, (1024, 512, 256, 128))
    c0 = col0 // tn
    return pl.pallas_call(
        _mm_kernel,
        grid=(m // tm, n // tn),
        in_specs=[pl.BlockSpec((tm, k), lambda i, j: (i, 0)),
                  pl.BlockSpec((k, tn), lambda i, j: (0, c0 + j))],
        out_specs=pl.BlockSpec((tm, tn), lambda i, j: (i, j)),
        out_shape=jax.ShapeDtypeStruct((m, n), out_dtype),
        compiler_params=_params("arbitrary", "arbitrary"),
        name="matmul",
    )(a, w)


def _subhead_rms(x, g, ones_bd):
    sq = x * x
    hi = sq.astype(BF16)
    lo = (sq - hi.astype(F32)).astype(BF16)
    ss = (jnp.dot(hi, ones_bd, preferred_element_type=F32)
          + jnp.dot(lo, ones_bd, preferred_element_type=F32))
    return x * lax.rsqrt(ss * (1.0 / HEAD_DIM) + 1e-6) * g


def _rope(x, cos, sin_signed, even):
    w = x.shape[-1]
    nxt = pltpu.roll(x, w - 1, 1)
    prv = pltpu.roll(x, 1, 1)
    return x * cos + jnp.where(even, nxt, prv) * sin_signed


def _qkv_post_kernel(q_ref, k_ref, v_ref, cos_ref, sin_ref, gq_ref, gk_ref, ones_ref,
                     qo_ref, ko_ref, vo_ref, *, nh, tk):
    ones_bd = ones_ref[...]
    cos = cos_ref[...]
    sin = sin_ref[...]
    tm = cos.shape[0]
    lane = lax.broadcasted_iota(jnp.int32, cos.shape, 1)
    even = (lane % 2) == 0
    sub1 = lax.broadcasted_iota(jnp.int32, (V_HEAD_DIM, tm), 0) < HEAD_DIM
    for h in range(nh):
        sl = slice(h * V_HEAD_DIM, (h + 1) * V_HEAD_DIM)
        q = _subhead_rms(q_ref[:, sl], gq_ref[...], ones_bd)
        qt = (_rope(q, cos, sin, even) * (HEAD_DIM ** -0.5 * LOG2E)).T
        qo_ref[0, h, 0, 0] = jnp.where(sub1, qt, 0.0).astype(BF16)
        qo_ref[0, h, 0, 1] = jnp.where(sub1, 0.0, qt).astype(BF16)
        k = _subhead_rms(k_ref[:, sl], gk_ref[...], ones_bd)
        ko_ref[0, h] = _rope(k, cos, sin, even).astype(BF16)
        for c in range(tm // tk):
            vo_ref[0, h, c] = v_ref[c * tk:(c + 1) * tk, sl].T.astype(BF16)


def _kv_post_kernel(k_ref, v_ref, gk_ref, ones_ref, ko_ref, vo_ref, *, nh, tk):
    ones_bd = ones_ref[...]
    tm = k_ref.shape[0]
    for h in range(nh):
        sl = slice(h * V_HEAD_DIM, (h + 1) * V_HEAD_DIM)
        ko_ref[0, h] = _subhead_rms(k_ref[:, sl], gk_ref[...], ones_bd).astype(BF16)
        for c in range(tm // tk):
            vo_ref[0, h, c] = v_ref[c * tk:(c + 1) * tk, sl].T.astype(BF16)


def _ones_blockdiag():
    r = jnp.arange(LANES) // HEAD_DIM
    return (r[:, None] == r[None, :]).astype(BF16)


def _rope_tables(t):
    rows = t // GRID_W
    row = jnp.broadcast_to(jnp.arange(rows)[:, None], (rows, GRID_W)).reshape(-1)
    col = jnp.broadcast_to(jnp.arange(GRID_W)[None, :], (rows, GRID_W)).reshape(-1)
    n_pairs_axis = HEAD_DIM // 4
    freqs = ROPE_BASE ** (-jnp.arange(n_pairs_axis, dtype=F32) / n_pairs_axis)
    ang = jnp.concatenate([row.astype(F32)[:, None] * freqs,
                           col.astype(F32)[:, None] * freqs], axis=-1)
    cos = jnp.repeat(jnp.cos(ang), 2, axis=-1)
    sin = jnp.repeat(jnp.sin(ang), 2, axis=-1)
    sign = jnp.where(jnp.arange(HEAD_DIM) % 2 == 0, -1.0, 1.0).astype(F32)
    reps = LANES // HEAD_DIM
    return jnp.tile(cos, (1, reps)), jnp.tile(sin * sign, (1, reps))


def _kv_block(t, n_ctx):
    return _pick(math.gcd(t, n_ctx), (256, 128))


def _qkv_post(p, b, t, n_heads, gq, gk, tk):
    qk_w = n_heads * V_HEAD_DIM
    nh = _pick(n_heads, (4, 2, 1))
    wblk = nh * V_HEAD_DIM
    tm = _pick(t, (512, 256, 128))
    nt = t // tm
    nq = qk_w // wblk
    nc = tm // tk
    cos, sin = _rope_tables(t)
    g2 = lambda g: jnp.tile(g, LANES // HEAD_DIM).reshape(1, LANES)
    row = lambda bi, i, j: bi * nt + i
    return pl.pallas_call(
        functools.partial(_qkv_post_kernel, nh=nh, tk=tk),
        grid=(b, nt, nq),
        in_specs=[pl.BlockSpec((tm, wblk), lambda bi, i, j: (row(bi, i, j), j)),
                  pl.BlockSpec((tm, wblk), lambda bi, i, j: (row(bi, i, j), nq + j)),
                  pl.BlockSpec((tm, wblk), lambda bi, i, j: (row(bi, i, j), 2 * nq + j)),
                  pl.BlockSpec((tm, LANES), lambda bi, i, j: (i, 0)),
                  pl.BlockSpec((tm, LANES), lambda bi, i, j: (i, 0)),
                  pl.BlockSpec((1, LANES), lambda bi, i, j: (0, 0)),
                  pl.BlockSpec((1, LANES), lambda bi, i, j: (0, 0)),
                  pl.BlockSpec((LANES, LANES), lambda bi, i, j: (0, 0))],
        out_specs=[pl.BlockSpec((1, nh, 1, 2, V_HEAD_DIM, tm), lambda bi, i, j: (bi, j, i, 0, 0, 0)),
                   pl.BlockSpec((1, nh, tm, LANES), lambda bi, i, j: (bi, j, i, 0)),
                   pl.BlockSpec((1, nh, nc, V_HEAD_DIM, tk), lambda bi, i, j: (bi, j, i, 0, 0))],
        out_shape=[jax.ShapeDtypeStruct((b, n_heads, nt, 2, V_HEAD_DIM, tm), BF16),
                   jax.ShapeDtypeStruct((b, n_heads, t, LANES), BF16),
                   jax.ShapeDtypeStruct((b, n_heads, t // tk, V_HEAD_DIM, tk), BF16)],
        compiler_params=_params("arbitrary", "arbitrary", "arbitrary"),
        name="qkv_post",
    )(p, p, p, cos, sin, g2(gq), g2(gk), _ones_blockdiag())


def _kv_post(pc, b, t, n_heads, gk, tk):
    qk_w = n_heads * V_HEAD_DIM
    nh = _pick(n_heads, (4, 2, 1))
    wblk = nh * V_HEAD_DIM
    tm = _pick(t, (256, 128))
    nt = t // tm
    nq = qk_w // wblk
    nc = tm // tk
    g2 = jnp.tile(gk, LANES // HEAD_DIM).reshape(1, LANES)
    return pl.pallas_call(
        functools.partial(_kv_post_kernel, nh=nh, tk=tk),
        grid=(b, nt, nq),
        in_specs=[pl.BlockSpec((tm, wblk), lambda bi, i, j: (bi * nt + i, j)),
                  pl.BlockSpec((tm, wblk), lambda bi, i, j: (bi * nt + i, nq + j)),
                  pl.BlockSpec((1, LANES), lambda bi, i, j: (0, 0)),
                  pl.BlockSpec((LANES, LANES), lambda bi, i, j: (0, 0))],
        out_specs=[pl.BlockSpec((1, nh, tm, LANES), lambda bi, i, j: (bi, j, i, 0)),
                   pl.BlockSpec((1, nh, nc, V_HEAD_DIM, tk), lambda bi, i, j: (bi, j, i, 0, 0))],
        out_shape=[jax.ShapeDtypeStruct((b, n_heads, t, LANES), BF16),
                   jax.ShapeDtypeStruct((b, n_heads, t // tk, V_HEAD_DIM, tk), BF16)],
        compiler_params=_params("arbitrary", "arbitrary", "arbitrary"),
        name="kv_post",
    )(pc, pc, g2, _ones_blockdiag())


def _attn_kernel(qt_ref, kx_ref, kc_ref, vtx_ref, vtc_ref, lq1_ref, lk1_ref, lq2_ref, lk2_ref,
                 g_ref, o_ref,
                 q2_ref, s_ref, p_ref, mx_ref, a_ref, m_ref, l_ref, acc_ref,
                 *, tq, tkw, sub, nstep, ntile, lam_init):
    assert nstep >= 2
    tk = tkw * sub
    nw = 2 * tq
    cw = _pick(nw, (4 * LANES, 2 * LANES, LANES))
    chunks = [slice(c * cw, (c + 1) * cw) for c in range(nw // cw)]
    acc_ref[...] = jnp.zeros_like(acc_ref)
    l_ref[...] = jnp.zeros_like(l_ref)
    m_ref[...] = jnp.zeros_like(m_ref)

    def load_q(i):
        q2_ref[:, :tq] = qt_ref[0, 0, i, 0]
        q2_ref[:, tq:] = qt_ref[0, 0, i, 1]

    nbx = vtx_ref.shape[2]
    n_lat = kx_ref.shape[2]
    head = tk - tkw

    def stage_p(t):
        on_ctx = t == nstep - 1
        vt_tail = jnp.where(on_ctx, vtc_ref[0, 0, 0],
                            vtx_ref[0, 0, jnp.minimum(t * sub + sub - 1, nbx - 1)])
        for cols in chunks:
            pv = None
            for j in range(sub):
                vt_blk = vtx_ref[0, 0, t * sub + j] if j < sub - 1 else vt_tail
                d = jnp.dot(vt_blk, p_ref[j * tkw:(j + 1) * tkw, cols],
                            preferred_element_type=F32)
                pv = d if pv is None else pv + d
            acc_ref[:, cols] = a_ref[:, cols] * acc_ref[:, cols] + pv

    def stage_s(first, par):
        m_prev = jnp.where(first, -jnp.inf, m_ref[...])
        m_new = jnp.maximum(m_prev, mx_ref[...])
        alpha = jnp.exp2(m_prev - m_new)
        for cols in chunks:
            p = jnp.exp2(s_ref[:, cols] - m_new[:, cols])
            l_ref[par, :, cols] = (alpha[:, cols] * l_ref[par, :, cols]
                                   + jnp.sum(p, axis=0, keepdims=True))
            p_ref[:, cols] = p.astype(BF16)
        m_ref[...] = m_new
        a_ref[...] = alpha

    def stage_q(t):
        on_ctx = t == nstep - 1
        tail0 = pl.multiple_of(jnp.minimum(t * tk + head, n_lat - tkw), tkw)
        k_tail = jnp.where(on_ctx, kc_ref[0, 0], kx_ref[0, 0, pl.ds(tail0, tkw), :])
        if head:
            k_head = kx_ref[0, 0, pl.ds(pl.multiple_of(t * tk, tkw), head), :]
        for cols in chunks:
            s = jnp.dot(k_tail, q2_ref[:, cols], preferred_element_type=F32)
            s_ref[head:, cols] = s
            mx = jnp.max(s, axis=0, keepdims=True)
            if head:
                s = jnp.dot(k_head, q2_ref[:, cols], preferred_element_type=F32)
                s_ref[:head, cols] = s
                mx = jnp.maximum(mx, jnp.max(s, axis=0, keepdims=True))
            mx_ref[:, cols] = mx

    def finalize(i):
        lam = (jnp.exp(jnp.sum(lq1_ref[...] * lk1_ref[...], axis=-1, keepdims=True))
               - jnp.exp(jnp.sum(lq2_ref[...] * lk2_ref[...], axis=-1, keepdims=True)) + lam_init)
        o12 = acc_ref[...] * (1.0 / l_ref[i % 2])
        o = o12[:, :tq] - lam * o12[:, tq:]
        ms = jnp.mean(o * o, axis=0, keepdims=True)
        y = o * lax.rsqrt(ms + 1e-6) * g_ref[...] * (1.0 - lam_init)
        o_ref[0, pl.ds(pl.multiple_of(i * tq, tq), tq), :] = y.T.astype(o_ref.dtype)

    load_q(0)
    stage_q(0)
    stage_s(True, 0)
    stage_q(1)

    def step(g, carry):
        iq = g // nstep
        tq_ = g - iq * nstep
        ts = (g - 1) % nstep
        ip = (g - 2) // nstep
        tp = (g - 2) - ip * nstep

        @pl.when(tq_ == 0)
        def _():
            load_q(iq)

        stage_p(tp)
        stage_s(ts == 0, ((g - 1) // nstep) % 2)
        stage_q(tq_)

        @pl.when(tp == nstep - 1)
        def _():
            finalize(ip)

        return carry

    lax.fori_loop(2, ntile * nstep, step, 0)

    stage_p(nstep - 2)
    stage_s(False, (ntile - 1) % 2)
    stage_p(nstep - 1)
    finalize(ntile - 1)


def _attention(qt, k_x, k_c, vt_x, vt_c, lq1, lk1, lq2, lk2, subln_g, lam_init):
    b, h, ntile, _, _, tq = qt.shape
    t = ntile * tq
    n_lat, n_ctx = k_x.shape[2], k_c.shape[2]
    nbx, _, tkw = vt_x.shape[2:]
    assert n_ctx == tkw and vt_c.shape[2] == 1
    nblk = nbx + 1
    sub = max(d for d in range(1, nblk) if nblk % d == 0 and d * tkw <= ATTN_MAX_TK)
    tk = sub * tkw
    vec = lambda a: a.reshape(1, -1)
    small = lambda n: pl.BlockSpec((1, n), lambda bi, hi: (0, 0))
    stat = pltpu.VMEM((1, 2 * tq), F32)
    return pl.pallas_call(
        functools.partial(_attn_kernel, tq=tq, tkw=tkw, sub=sub, nstep=nblk // sub, ntile=ntile,
                          lam_init=lam_init),
        grid=(b, h),
        in_specs=[pl.BlockSpec((1, 1, ntile, 2, V_HEAD_DIM, tq), lambda bi, hi: (bi, hi, 0, 0, 0, 0)),
                  pl.BlockSpec((1, 1, n_lat, V_HEAD_DIM), lambda bi, hi: (bi, hi, 0, 0)),
                  pl.BlockSpec((1, 1, n_ctx, V_HEAD_DIM), lambda bi, hi: (bi, hi, 0, 0)),
                  pl.BlockSpec((1, 1, nbx, V_HEAD_DIM, tkw), lambda bi, hi: (bi, hi, 0, 0, 0)),
                  pl.BlockSpec((1, 1, 1, V_HEAD_DIM, tkw), lambda bi, hi: (bi, hi, 0, 0, 0)),
                  small(HEAD_DIM), small(HEAD_DIM), small(HEAD_DIM), small(HEAD_DIM),
                  pl.BlockSpec((V_HEAD_DIM, 1), lambda bi, hi: (0, 0))],
        out_specs=pl.BlockSpec((1, t, V_HEAD_DIM), lambda bi, hi: (bi, 0, hi)),
        out_shape=jax.ShapeDtypeStruct((b, t, h * V_HEAD_DIM), BF16),
        scratch_shapes=[pltpu.VMEM((V_HEAD_DIM, 2 * tq), BF16),
                        pltpu.VMEM((tk, 2 * tq), F32), pltpu.VMEM((tk, 2 * tq), BF16),
                        stat, stat, stat, pltpu.VMEM((2, 1, 2 * tq), F32),
                        pltpu.VMEM((V_HEAD_DIM, 2 * tq), F32)],
        compiler_params=_params("arbitrary", "arbitrary"),
        name="attention",
    )(qt, k_x, k_c, vt_x, vt_c, vec(lq1), vec(lk1), vec(lq2), vec(lk2), subln_g.reshape(-1, 1))


def _conv_kernel(ca_ref, cg_ref, cap_ref, cgp_ref, can_ref, cgn_ref, w_ref, b_ref, lg_ref, lb_ref,
                 o_ref, u_ref, y_ref, *, tm, kw, rc):
    i = pl.program_id(1)
    pad = (kw - 1) // 2
    glu = lambda a, g: a * jax.nn.sigmoid(g)
    first = i == 0
    last = i == pl.num_programs(1) - 1
    u_ref[0:HALO] = jnp.where(first, 0.0, glu(cap_ref[...], cgp_ref[...]))
    u_ref[HALO:HALO + tm] = glu(ca_ref[...], cg_ref[...])
    u_ref[HALO + tm:] = jnp.where(last, 0.0, glu(can_ref[...], cgn_ref[...]))
    nstrip = u_ref.shape[1] // LANES

    def strip(c, carry):
        lanes = pl.ds(pl.multiple_of(c * LANES, LANES), LANES)
        for r in range(tm // rc):
            span = HALO - pad + kw - 1
            nwin = rc + 8 * (span // 8 + 1)
            win = u_ref[r * rc:r * rc + nwin, lanes]
            acc = jnp.zeros((rc, LANES), F32)
            for rho in range(8):
                offs = [o for o in range(HALO - pad, span + 1) if o % 8 == rho]
                if not offs:
                    continue
                sh = pltpu.roll(win, nwin - rho, 0) if rho else win
                for o in offs:
                    k = o - (HALO - pad)
                    q8 = o - rho
                    acc = acc + sh[q8:q8 + rc] * w_ref[k:k + 1, lanes]
            y_ref[r * rc:(r + 1) * rc, lanes] = acc + b_ref[:, lanes]
        return carry

    lax.fori_loop(0, nstrip, strip, 0)
    y = y_ref[...]
    mu = jnp.mean(y, axis=-1, keepdims=True)
    yc = y - mu
    var = jnp.mean(yc * yc, axis=-1, keepdims=True)
    z = yc * lax.rsqrt(var + 1e-5) * lg_ref[...] + lb_ref[...]
    o_ref[...] = (z * jax.nn.sigmoid(z)).astype(o_ref.dtype)


def _conformer_conv(p, b, t, col_a, col_g, cw, dw_w, dw_b, ln_g, ln_b):
    kw = dw_w.shape[0]
    tm = _pick(t, (256, 128))
    nt = t // tm
    hb = tm // HALO
    nrow_h = (b * t) // HALO
    ja, jg = col_a // cw, col_g // cw
    cur = lambda jc: pl.BlockSpec((tm, cw), lambda bi, i: (bi * nt + i, jc))
    prev = lambda jc: pl.BlockSpec(
        (HALO, cw), lambda bi, i: (jnp.maximum((bi * nt + i) * hb - 1, 0), jc))
    nxt = lambda jc: pl.BlockSpec(
        (HALO, cw), lambda bi, i: (jnp.minimum((bi * nt + i + 1) * hb, nrow_h - 1), jc))
    vec = lambda n: pl.BlockSpec((1, n), lambda bi, i: (0, 0))
    return pl.pallas_call(
        functools.partial(_conv_kernel, tm=tm, kw=kw, rc=64),
        grid=(b, nt),
        in_specs=[cur(ja), cur(jg), prev(ja), prev(jg), nxt(ja), nxt(jg),
                  pl.BlockSpec((kw, cw), lambda bi, i: (0, 0)), vec(cw), vec(cw), vec(cw)],
        out_specs=pl.BlockSpec((tm, cw), lambda bi, i: (bi * nt + i, 0)),
        out_shape=jax.ShapeDtypeStruct((b * t, cw), BF16),
        scratch_shapes=[pltpu.VMEM((tm + 2 * HALO, cw), F32), pltpu.VMEM((tm, cw), F32)],
        compiler_params=_params("arbitrary", "arbitrary"),
        name="conformer_conv",
    )(p, p, p, p, p, p, dw_w, dw_b.reshape(1, cw), ln_g.reshape(1, cw), ln_b.reshape(1, cw))


def _out_proj_kernel(att_ref, conv_ref, wa_ref, wc_ref, x_ref, gate_ref, o_ref):
    mix = (jnp.dot(att_ref[...], wa_ref[...], preferred_element_type=F32)
           + jnp.dot(conv_ref[...], wc_ref[...], preferred_element_type=F32))
    o_ref[...] = x_ref[...] + gate_ref[0] * mix


def _out_proj(att, conv, w_out, x2d, gate, b, t):
    m, ka = att.shape
    d = w_out.shape[1]
    tm = _pick(t, (1024, 512, 256, 128))
    tn = _pick(d, (1024, 512, 256, 128))
    nt = t // tm
    return pl.pallas_call(
        _out_proj_kernel,
        grid=(m // tm, d // tn),
        in_specs=[pl.BlockSpec((tm, ka), lambda i, j: (i, 0)),
                  pl.BlockSpec((tm, ka), lambda i, j: (i, 0)),
                  pl.BlockSpec((ka, tn), lambda i, j: (0, j)),
                  pl.BlockSpec((ka, tn), lambda i, j: (1, j)),
                  pl.BlockSpec((tm, tn), lambda i, j: (i, j)),
                  pl.BlockSpec((1, 1, tn), lambda i, j: (i // nt, 0, j))],
        out_specs=pl.BlockSpec((tm, tn), lambda i, j: (i, j)),
        out_shape=jax.ShapeDtypeStruct((m, d), F32),
        compiler_params=_params("arbitrary", "arbitrary"),
        name="out_proj",
    )(att, conv, w_out, w_out, x2d, gate)


def _ffn_up_kernel(h_ref, hp_ref, hn_ref, *refs, tm, nt, nchunk):
    o_ref, a_ref = refs[6 * nchunk:]
    i = pl.program_id(0)

    @pl.when(pl.program_id(1) == 0)
    def _():
        first = (i % nt) == 0
        last = (i % nt) == nt - 1
        a_ref[0:HALO] = jnp.where(first, jnp.zeros_like(hp_ref), hp_ref[...])
        a_ref[HALO:HALO + tm] = h_ref[...]
        a_ref[HALO + tm:] = jnp.where(last, jnp.zeros_like(hn_ref), hn_ref[...])

    ext = tm + 2 * HALO

    def conv3(w_ref, cw_ref, cb_ref):
        up = jnp.dot(a_ref[...], w_ref[...], preferred_element_type=F32)
        dn = pltpu.roll(up, 1, 0)[HALO:HALO + tm]
        nx = pltpu.roll(up, ext - 1, 0)[HALO:HALO + tm]
        return (dn * cw_ref[0:1] + up[HALO:HALO + tm] * cw_ref[1:2] + nx * cw_ref[2:3]
                + cb_ref[...])

    for c in range(nchunk):
        wu_ref, wg_ref, cwu_ref, cwg_ref, cbu_ref, cbg_ref = refs[6 * c:6 * c + 6]
        tc = wu_ref.shape[1]
        u = conv3(wu_ref, cwu_ref, cbu_ref)
        g = conv3(wg_ref, cwg_ref, cbg_ref)
        o_ref[:, c * tc:(c + 1) * tc] = (g * jax.nn.sigmoid(g) * u).astype(o_ref.dtype)


def _ffn_up(h2, w_up, dw_w, dw_b, b, t):
    m, d = h2.shape
    f = w_up.shape[1] // 2
    tm = _pick(t, (1024, 512, 256, 128))
    tc = _pick(f, (256, 128))
    nchunk = 2
    tn = nchunk * tc
    nt = t // tm
    hb = tm // HALO
    nrow_h = m // HALO
    nfc = f // tc
    kw = dw_w.shape[0]
    db = dw_b.reshape(1, -1)
    col = lambda half, c: (lambda i, j: (0, jnp.minimum(half * nfc + j * nchunk + c, 2 * nfc - 1)))
    specs, args = [], []
    for c in range(nchunk):
        for rows, arr in ((d, w_up), (kw, dw_w), (1, db)):
            for half in (0, 1):
                specs.append(pl.BlockSpec((rows, tc), col(half, c)))
                args.append(arr)
    return pl.pallas_call(
        functools.partial(_ffn_up_kernel, tm=tm, nt=nt, nchunk=nchunk),
        grid=(m // tm, pl.cdiv(f, tn)),
        in_specs=[pl.BlockSpec((tm, d), lambda i, j: (i, 0)),
                  pl.BlockSpec((HALO, d), lambda i, j: (jnp.maximum(i * hb - 1, 0), 0)),
                  pl.BlockSpec((HALO, d), lambda i, j: (jnp.minimum((i + 1) * hb, nrow_h - 1), 0))]
                 + specs,
        out_specs=pl.BlockSpec((tm, tn), lambda i, j: (i, j)),
        out_shape=jax.ShapeDtypeStruct((m, f), BF16),
        scratch_shapes=[pltpu.VMEM((tm + 2 * HALO, d), BF16)],
        compiler_params=_params("arbitrary", "arbitrary"),
        name="ffn_up",
    )(h2, h2, h2, *args)


def _ffn_down_kernel(a_ref, w_ref, x_ref, gate_ref, o_ref):
    y = jnp.dot(a_ref[...], w_ref[...], preferred_element_type=F32)
    o_ref[...] = x_ref[...] + gate_ref[0] * y


def _ffn_down(act, w_down, x1, gate, t):
    m, f = act.shape
    d = w_down.shape[1]
    tm = _pick(t, (512, 256, 128))
    tn = _pick(d, (512, 256, 128))
    nt = t // tm
    return pl.pallas_call(
        _ffn_down_kernel,
        grid=(m // tm, d // tn),
        in_specs=[pl.BlockSpec((tm, f), lambda i, j: (i, 0)),
                  pl.BlockSpec((f, tn), lambda i, j: (0, j)),
                  pl.BlockSpec((tm, tn), lambda i, j: (i, j)),
                  pl.BlockSpec((1, 1, tn), lambda i, j: (i // nt, 0, j))],
        out_specs=pl.BlockSpec((tm, tn), lambda i, j: (i, j)),
        out_shape=jax.ShapeDtypeStruct((m, d), F32),
        compiler_params=_params("arbitrary", "arbitrary"),
        name="ffn_down",
    )(act, w_down, x1, gate)


def _layer(x, ctx, mods, lam_init, norm1_g, norm2_g, w_in, q_norm_g, k_norm_g, lq1, lk1, lq2, lk2,
           subln_g, conv_dw_w, conv_dw_b, conv_ln_g, conv_ln_b, w_out, w_up, ffn_dw_w, ffn_dw_b,
           w_down):
    b, t, d = x.shape
    n_ctx = ctx.shape[1]
    mix_w = w_out.shape[0]
    att_w = mix_w // 2
    conv_w = mix_w - att_w
    n_heads = att_w // V_HEAD_DIM
    qk_w = n_heads * 2 * HEAD_DIM
    v_w = n_heads * V_HEAD_DIM

    mod_x = mods[:b].reshape(b, N_MOD, 1, d)
    mod_c = jnp.broadcast_to(mods[b].reshape(1, N_MOD, 1, d), (b, N_MOD, 1, d))

    w_in_b = w_in.astype(BF16)
    h_x = _norm_mod(x, norm1_g, mod_x[:, 0], mod_x[:, 1]).reshape(b * t, d)
    h_c = _norm_mod(ctx, norm1_g, mod_c[:, 0], mod_c[:, 1]).reshape(b * n_ctx, d)

    p_x = _matmul(h_x, w_in_b, F32)
    p_c = _matmul(h_c, w_in_b, F32, col0=qk_w, ncols=qk_w + v_w)

    tk = _kv_block(t, n_ctx)
    qt, k_x, vt_x = _qkv_post(p_x, b, t, n_heads, q_norm_g, k_norm_g, tk)
    k_c, vt_c = _kv_post(p_c, b, n_ctx, n_heads, k_norm_g, tk)
    att = _attention(qt, k_x, k_c, vt_x, vt_c, lq1, lk1, lq2, lk2, subln_g,
                     lam_init).reshape(b * t, v_w)

    conv = _conformer_conv(p_x, b, t, 2 * qk_w + v_w, 2 * qk_w + v_w + conv_w, conv_w,
                           conv_dw_w, conv_dw_b, conv_ln_g, conv_ln_b)

    x1 = _out_proj(att, conv, w_out.astype(BF16), x.reshape(b * t, d), mod_x[:, 2], b, t)
    h2 = _norm_mod(x1.reshape(b, t, d), norm2_g, mod_x[:, 3], mod_x[:, 4]).reshape(b * t, d)
    act = _ffn_up(h2, w_up.astype(BF16), ffn_dw_w, ffn_dw_b, b, t)
    out = _ffn_down(act, w_down.astype(BF16), x1, mod_x[:, 5], t)
    return out.reshape(b, t, d)


def kernel(x, c, ctx, c_ctx, w_ada, b_ada, norm1_g, norm2_g, w_in, q_norm_g, k_norm_g, lambda_q1,
           lambda_k1, lambda_q2, lambda_k2, subln_g, conv_dw_w, conv_dw_b, conv_ln_g, conv_ln_b,
           w_out, w_up, ffn_dw_w, ffn_dw_b, w_down):
    b, t, d = x.shape
    depth = w_ada.shape[0]
    assert depth == 1, "context-stream updates are only needed for depth > 1"
    i = 0
    lam_init = 0.8 - 0.6 * math.exp(-0.3 * i)
    cc = jnp.zeros((8, d), F32).at[:b].set(c).at[b].set(c_ctx)
    mods = _ada(cc, w_ada[i], b_ada[i])
    return _layer(x, ctx, mods, lam_init, norm1_g[i], norm2_g[i], w_in[i], q_norm_g[i],
                  k_norm_g[i], lambda_q1[i], lambda_k1[i], lambda_q2[i], lambda_k2[i], subln_g[i],
                  conv_dw_w[i], conv_dw_b[i], conv_ln_g[i], conv_ln_b[i], w_out[i], w_up[i],
                  ffn_dw_w[i], ffn_dw_b[i], w_down[i])
```

```python
import functools
import math

import jax
import jax.numpy as jnp
from jax import lax
from jax.experimental import pallas as pl
from jax.experimental.pallas import tpu as pltpu

F32 = jnp.float32
BF16 = jnp.bfloat16

HEAD_DIM = 64
V_HEAD_DIM = 2 * HEAD_DIM
GRID_W = 64
ROPE_BASE = 10000.0
N_MOD = 6
LOG2E = math.log2(math.e)
ATTN_MAX_TK = 2816
LANES = 128
HALO = 16
VMEM_LIMIT = 56 * 1024 * 1024


def _params(*sem, flags=None):
    return pltpu.CompilerParams(dimension_semantics=sem, vmem_limit_bytes=VMEM_LIMIT, flags=flags)


def _pick(n, prefs):
    for p in prefs:
        if n % p == 0:
            return p
    return n


def _ada_kernel(c_ref, w_ref, b_ref, o_ref):
    c = c_ref[...]
    s = (c * jax.nn.sigmoid(c)).astype(BF16)
    o_ref[...] = jnp.dot(s, w_ref[...].astype(BF16), preferred_element_type=F32) + b_ref[...]


def _ada(cc, w_ada, b_ada):
    d, n = w_ada.shape
    tn = _pick(n, (512, 256, 128))
    return pl.pallas_call(
        _ada_kernel,
        grid=(n // tn,),
        in_specs=[pl.BlockSpec((8, d), lambda j: (0, 0)),
                  pl.BlockSpec((d, tn), lambda j: (0, j)),
                  pl.BlockSpec((1, tn), lambda j: (0, j))],
        out_specs=pl.BlockSpec((8, tn), lambda j: (0, j)),
        out_shape=jax.ShapeDtypeStruct((8, n), F32),
        compiler_params=_params("arbitrary"),
        name="ada",
    )(cc, w_ada, b_ada.reshape(1, n))


def _norm_mod_kernel(x_ref, g_ref, shift_ref, scale_ref, o_ref):
    x = x_ref[0]
    ms = jnp.mean(x * x, axis=-1, keepdims=True)
    y = x * lax.rsqrt(ms + 1e-6) * g_ref[...]
    o_ref[0] = (y * (1.0 + scale_ref[0]) + shift_ref[0]).astype(o_ref.dtype)


def _norm_mod(x, g, shift, scale):
    b, t, d = x.shape
    tm = _pick(t, (512, 256, 128))
    return pl.pallas_call(
        _norm_mod_kernel,
        grid=(b, t // tm),
        in_specs=[pl.BlockSpec((1, tm, d), lambda bi, i: (bi, i, 0)),
                  pl.BlockSpec((1, d), lambda bi, i: (0, 0)),
                  pl.BlockSpec((1, 1, d), lambda bi, i: (bi, 0, 0)),
                  pl.BlockSpec((1, 1, d), lambda bi, i: (bi, 0, 0))],
        out_specs=pl.BlockSpec((1, tm, d), lambda bi, i: (bi, i, 0)),
        out_shape=jax.ShapeDtypeStruct((b, t, d), BF16),
        compiler_params=_params("arbitrary", "arbitrary"),
        name="norm_mod",
    )(x, g.reshape(1, d), shift, scale)


def _mm_kernel(a_ref, b_ref, o_ref):
    o_ref[...] = jnp.dot(a_ref[...], b_ref[...], preferred_element_type=F32).astype(o_ref.dtype)


def _matmul(a, w, out_dtype, col0=0, ncols=None):
    m, k = a.shape
    n = w.shape[1] if ncols is None else ncols
    tm = _pick(m, (1024, 512, 256, 128))
    tn = _pick(math.gcd(n, col0) if col0 else n, (1024, 512, 256, 128))
    c0 = col0 // tn
    return pl.pallas_call(
        _mm_kernel,
        grid=(m // tm, n // tn),
        in_specs=[pl.BlockSpec((tm, k), lambda i, j: (i, 0)),
                  pl.BlockSpec((k, tn), lambda i, j: (0, c0 + j))],
        out_specs=pl.BlockSpec((tm, tn), lambda i, j: (i, j)),
        out_shape=jax.ShapeDtypeStruct((m, n), out_dtype),
        compiler_params=_params("arbitrary", "arbitrary"),
        name="matmul",
    )(a, w)


def _subhead_rms(x, g, ones_bd):
    sq = x * x
    hi = sq.astype(BF16)
    lo = (sq - hi.astype(F32)).astype(BF16)
    ss = (jnp.dot(hi, ones_bd, preferred_element_type=F32)
          + jnp.dot(lo, ones_bd, preferred_element_type=F32))
    return x * lax.rsqrt(ss * (1.0 / HEAD_DIM) + 1e-6) * g


def _rope(x, cos, sin_signed, even):
    w = x.shape[-1]
    nxt = pltpu.roll(x, w - 1, 1)
    prv = pltpu.roll(x, 1, 1)
    return x * cos + jnp.where(even, nxt, prv) * sin_signed


def _qkv_post_kernel(q_ref, k_ref, v_ref, cos_ref, sin_ref, gq_ref, gk_ref, ones_ref,
                     qo_ref, ko_ref, vo_ref, *, nh, tk):
    ones_bd = ones_ref[...]
    cos = cos_ref[...]
    sin = sin_ref[...]
    tm = cos.shape[0]
    lane = lax.broadcasted_iota(jnp.int32, cos.shape, 1)
    even = (lane % 2) == 0
    sub1 = lax.broadcasted_iota(jnp.int32, (V_HEAD_DIM, tm), 0) < HEAD_DIM
    for h in range(nh):
        sl = slice(h * V_HEAD_DIM, (h + 1) * V_HEAD_DIM)
        q = _subhead_rms(q_ref[:, sl], gq_ref[...], ones_bd)
        qt = (_rope(q, cos, sin, even) * (HEAD_DIM ** -0.5 * LOG2E)).T
        qo_ref[0, h, 0, 0] = jnp.where(sub1, qt, 0.0).astype(BF16)
        qo_ref[0, h, 0, 1] = jnp.where(sub1, 0.0, qt).astype(BF16)
        k = _subhead_rms(k_ref[:, sl], gk_ref[...], ones_bd)
        ko_ref[0, h] = _rope(k, cos, sin, even).astype(BF16)
        for c in range(tm // tk):
            vo_ref[0, h, c] = v_ref[c * tk:(c + 1) * tk, sl].T.astype(BF16)


def _kv_post_kernel(k_ref, v_ref, gk_ref, ones_ref, ko_ref, vo_ref, *, nh, tk):
    ones_bd = ones_ref[...]
    tm = k_ref.shape[0]
    for h in range(nh):
        sl = slice(h * V_HEAD_DIM, (h + 1) * V_HEAD_DIM)
        ko_ref[0, h] = _subhead_rms(k_ref[:, sl], gk_ref[...], ones_bd).astype(BF16)
        for c in range(tm // tk):
            vo_ref[0, h, c] = v_ref[c * tk:(c + 1) * tk, sl].T.astype(BF16)


def _ones_blockdiag():
    r = jnp.arange(LANES) // HEAD_DIM
    return (r[:, None] == r[None, :]).astype(BF16)


def _rope_tables(t):
    rows = t // GRID_W
    row = jnp.broadcast_to(jnp.arange(rows)[:, None], (rows, GRID_W)).reshape(-1)
    col = jnp.broadcast_to(jnp.arange(GRID_W)[None, :], (rows, GRID_W)).reshape(-1)
    n_pairs_axis = HEAD_DIM // 4
    freqs = ROPE_BASE ** (-jnp.arange(n_pairs_axis, dtype=F32) / n_pairs_axis)
    ang = jnp.concatenate([row.astype(F32)[:, None] * freqs,
                           col.astype(F32)[:, None] * freqs], axis=-1)
    cos = jnp.repeat(jnp.cos(ang), 2, axis=-1)
    sin = jnp.repeat(jnp.sin(ang), 2, axis=-1)
    sign = jnp.where(jnp.arange(HEAD_DIM) % 2 == 0, -1.0, 1.0).astype(F32)
    reps = LANES // HEAD_DIM
    return jnp.tile(cos, (1, reps)), jnp.tile(sin * sign, (1, reps))


def _kv_block(t, n_ctx):
    return _pick(math.gcd(t, n_ctx), (256, 128))


def _qkv_post(p, b, t, n_heads, gq, gk, tk):
    qk_w = n_heads * V_HEAD_DIM
    nh = _pick(n_heads, (4, 2, 1))
    wblk = nh * V_HEAD_DIM
    tm = _pick(t, (512, 256, 128))
    nt = t // tm
    nq = qk_w // wblk
    nc = tm // tk
    cos, sin = _rope_tables(t)
    g2 = lambda g: jnp.tile(g, LANES // HEAD_DIM).reshape(1, LANES)
    row = lambda bi, i, j: bi * nt + i
    return pl.pallas_call(
        functools.partial(_qkv_post_kernel, nh=nh, tk=tk),
        grid=(b, nt, nq),
        in_specs=[pl.BlockSpec((tm, wblk), lambda bi, i, j: (row(bi, i, j), j)),
                  pl.BlockSpec((tm, wblk), lambda bi, i, j: (row(bi, i, j), nq + j)),
                  pl.BlockSpec((tm, wblk), lambda bi, i, j: (row(bi, i, j), 2 * nq + j)),
                  pl.BlockSpec((tm, LANES), lambda bi, i, j: (i, 0)),
                  pl.BlockSpec((tm, LANES), lambda bi, i, j: (i, 0)),
                  pl.BlockSpec((1, LANES), lambda bi, i, j: (0, 0)),
                  pl.BlockSpec((1, LANES), lambda bi, i, j: (0, 0)),
                  pl.BlockSpec((LANES, LANES), lambda bi, i, j: (0, 0))],
        out_specs=[pl.BlockSpec((1, nh, 1, 2, V_HEAD_DIM, tm), lambda bi, i, j: (bi, j, i, 0, 0, 0)),
                   pl.BlockSpec((1, nh, tm, LANES), lambda bi, i, j: (bi, j, i, 0)),
                   pl.BlockSpec((1, nh, nc, V_HEAD_DIM, tk), lambda bi, i, j: (bi, j, i, 0, 0))],
        out_shape=[jax.ShapeDtypeStruct((b, n_heads, nt, 2, V_HEAD_DIM, tm), BF16),
                   jax.ShapeDtypeStruct((b, n_heads, t, LANES), BF16),
                   jax.ShapeDtypeStruct((b, n_heads, t // tk, V_HEAD_DIM, tk), BF16)],
        compiler_params=_params("arbitrary", "arbitrary", "arbitrary"),
        name="qkv_post",
    )(p, p, p, cos, sin, g2(gq), g2(gk), _ones_blockdiag())


def _kv_post(pc, b, t, n_heads, gk, tk):
    qk_w = n_heads * V_HEAD_DIM
    nh = _pick(n_heads, (4, 2, 1))
    wblk = nh * V_HEAD_DIM
    tm = _pick(t, (256, 128))
    nt = t // tm
    nq = qk_w // wblk
    nc = tm // tk
    g2 = jnp.tile(gk, LANES // HEAD_DIM).reshape(1, LANES)
    return pl.pallas_call(
        functools.partial(_kv_post_kernel, nh=nh, tk=tk),
        grid=(b, nt, nq),
        in_specs=[pl.BlockSpec((tm, wblk), lambda bi, i, j: (bi * nt + i, j)),
                  pl.BlockSpec((tm, wblk), lambda bi, i, j: (bi * nt + i, nq + j)),
                  pl.BlockSpec((1, LANES), lambda bi, i, j: (0, 0)),
                  pl.BlockSpec((LANES, LANES), lambda bi, i, j: (0, 0))],
        out_specs=[pl.BlockSpec((1, nh, tm, LANES), lambda bi, i, j: (bi, j, i, 0)),
                   pl.BlockSpec((1, nh, nc, V_HEAD_DIM, tk), lambda bi, i, j: (bi, j, i, 0, 0))],
        out_shape=[jax.ShapeDtypeStruct((b, n_heads, t, LANES), BF16),
                   jax.ShapeDtypeStruct((b, n_heads, t // tk, V_HEAD_DIM, tk), BF16)],
        compiler_params=_params("arbitrary", "arbitrary", "arbitrary"),
        name="kv_post",
    )(pc, pc, g2, _ones_blockdiag())


def _attn_kernel(qt_ref, k_ref, vt_ref, lq1_ref, lk1_ref, lq2_ref, lk2_ref, g_ref, o_ref,
                 q2_ref, s_ref, p_ref, mx_ref, a_ref, m_ref, l_ref, acc_ref,
                 *, tq, tkw, sub, nstep, ntile, lam_init):
    assert nstep >= 2
    tk = tkw * sub
    nw = 2 * tq
    cw = _pick(nw, (4 * LANES, 2 * LANES, LANES))
    chunks = [slice(c * cw, (c + 1) * cw) for c in range(nw // cw)]
    acc_ref[...] = jnp.zeros_like(acc_ref)
    l_ref[...] = jnp.zeros_like(l_ref)
    m_ref[...] = jnp.zeros_like(m_ref)

    def load_q(i):
        q2_ref[:, :tq] = qt_ref[0, 0, i, 0]
        q2_ref[:, tq:] = qt_ref[0, 0, i, 1]

    def stage_p(t):
        for cols in chunks:
            pv = None
            for j in range(sub):
                d = jnp.dot(vt_ref[0, 0, t * sub + j], p_ref[j * tkw:(j + 1) * tkw, cols],
                            preferred_element_type=F32)
                pv = d if pv is None else pv + d
            acc_ref[:, cols] = a_ref[:, cols] * acc_ref[:, cols] + pv

    def stage_s(first, par):
        m_prev = jnp.where(first, -jnp.inf, m_ref[...])
        m_new = jnp.maximum(m_prev, mx_ref[...])
        alpha = jnp.exp2(m_prev - m_new)
        for cols in chunks:
            p = jnp.exp2(s_ref[:, cols] - m_new[:, cols])
            l_ref[par, :, cols] = (alpha[:, cols] * l_ref[par, :, cols]
                                   + jnp.sum(p, axis=0, keepdims=True))
            p_ref[:, cols] = p.astype(BF16)
        m_ref[...] = m_new
        a_ref[...] = alpha

    def stage_q(t):
        kb = k_ref[0, 0, pl.ds(pl.multiple_of(t * tk, tk), tk), :]
        for cols in chunks:
            s = jnp.dot(kb, q2_ref[:, cols], preferred_element_type=F32)
            s_ref[:, cols] = s
            mx_ref[:, cols] = jnp.max(s, axis=0, keepdims=True)

    def finalize(i):
        lam = (jnp.exp(jnp.sum(lq1_ref[...] * lk1_ref[...], axis=-1, keepdims=True))
               - jnp.exp(jnp.sum(lq2_ref[...] * lk2_ref[...], axis=-1, keepdims=True)) + lam_init)
        o12 = acc_ref[...] * (1.0 / l_ref[i % 2])
        o = o12[:, :tq] - lam * o12[:, tq:]
        ms = jnp.mean(o * o, axis=0, keepdims=True)
        y = o * lax.rsqrt(ms + 1e-6) * g_ref[...] * (1.0 - lam_init)
        o_ref[0, pl.ds(pl.multiple_of(i * tq, tq), tq), :] = y.T.astype(o_ref.dtype)

    load_q(0)
    stage_q(0)
    stage_s(True, 0)
    stage_q(1)

    def step(g, carry):
        iq = g // nstep
        tq_ = g - iq * nstep
        ts = (g - 1) % nstep
        ip = (g - 2) // nstep
        tp = (g - 2) - ip * nstep

        @pl.when(tq_ == 0)
        def _():
            load_q(iq)

        stage_p(tp)
        stage_s(ts == 0, ((g - 1) // nstep) % 2)
        stage_q(tq_)

        @pl.when(tp == nstep - 1)
        def _():
            finalize(ip)

        return carry

    lax.fori_loop(2, ntile * nstep, step, 0)

    stage_p(nstep - 2)
    stage_s(False, (ntile - 1) % 2)
    stage_p(nstep - 1)
    finalize(ntile - 1)


def _attention(qt, k, vt, lq1, lk1, lq2, lk2, subln_g, lam_init):
    b, h, ntile, _, _, tq = qt.shape
    t = ntile * tq
    l = k.shape[2]
    nblk, _, tkw = vt.shape[2:]
    sub = max(d for d in range(1, nblk) if nblk % d == 0 and d * tkw <= ATTN_MAX_TK)
    tk = sub * tkw
    vec = lambda a: a.reshape(1, -1)
    small = lambda n: pl.BlockSpec((1, n), lambda bi, hi: (0, 0))
    stat = pltpu.VMEM((1, 2 * tq), F32)
    return pl.pallas_call(
        functools.partial(_attn_kernel, tq=tq, tkw=tkw, sub=sub, nstep=nblk // sub, ntile=ntile,
                          lam_init=lam_init),
        grid=(b, h),
        in_specs=[pl.BlockSpec((1, 1, ntile, 2, V_HEAD_DIM, tq), lambda bi, hi: (bi, hi, 0, 0, 0, 0)),
                  pl.BlockSpec((1, 1, l, V_HEAD_DIM), lambda bi, hi: (bi, hi, 0, 0)),
                  pl.BlockSpec((1, 1, nblk, V_HEAD_DIM, tkw), lambda bi, hi: (bi, hi, 0, 0, 0)),
                  small(HEAD_DIM), small(HEAD_DIM), small(HEAD_DIM), small(HEAD_DIM),
                  pl.BlockSpec((V_HEAD_DIM, 1), lambda bi, hi: (0, 0))],
        out_specs=pl.BlockSpec((1, t, V_HEAD_DIM), lambda bi, hi: (bi, 0, hi)),
        out_shape=jax.ShapeDtypeStruct((b, t, h * V_HEAD_DIM), BF16),
        scratch_shapes=[pltpu.VMEM((V_HEAD_DIM, 2 * tq), BF16),
                        pltpu.VMEM((tk, 2 * tq), F32), pltpu.VMEM((tk, 2 * tq), BF16),
                        stat, stat, stat, pltpu.VMEM((2, 1, 2 * tq), F32),
                        pltpu.VMEM((V_HEAD_DIM, 2 * tq), F32)],
        compiler_params=_params("arbitrary", "arbitrary"),
        name="attention",
    )(qt, k, vt, vec(lq1), vec(lk1), vec(lq2), vec(lk2), subln_g.reshape(-1, 1))


def _conv_kernel(ca_ref, cg_ref, cap_ref, cgp_ref, can_ref, cgn_ref, w_ref, b_ref, lg_ref, lb_ref,
                 o_ref, u_ref, y_ref, *, tm, kw, rc):
    i = pl.program_id(1)
    pad = (kw - 1) // 2
    glu = lambda a, g: a * jax.nn.sigmoid(g)
    first = i == 0
    last = i == pl.num_programs(1) - 1
    u_ref[0:HALO] = jnp.where(first, 0.0, glu(cap_ref[...], cgp_ref[...]))
    u_ref[HALO:HALO + tm] = glu(ca_ref[...], cg_ref[...])
    u_ref[HALO + tm:] = jnp.where(last, 0.0, glu(can_ref[...], cgn_ref[...]))
    nstrip = u_ref.shape[1] // LANES

    def strip(c, carry):
        lanes = pl.ds(pl.multiple_of(c * LANES, LANES), LANES)
        for r in range(tm // rc):
            span = HALO - pad + kw - 1
            nwin = rc + 8 * (span // 8 + 1)
            win = u_ref[r * rc:r * rc + nwin, lanes]
            acc = jnp.zeros((rc, LANES), F32)
            for rho in range(8):
                offs = [o for o in range(HALO - pad, span + 1) if o % 8 == rho]
                if not offs:
                    continue
                sh = pltpu.roll(win, nwin - rho, 0) if rho else win
                for o in offs:
                    k = o - (HALO - pad)
                    q8 = o - rho
                    acc = acc + sh[q8:q8 + rc] * w_ref[k:k + 1, lanes]
            y_ref[r * rc:(r + 1) * rc, lanes] = acc + b_ref[:, lanes]
        return carry

    lax.fori_loop(0, nstrip, strip, 0)
    y = y_ref[...]
    mu = jnp.mean(y, axis=-1, keepdims=True)
    yc = y - mu
    var = jnp.mean(yc * yc, axis=-1, keepdims=True)
    z = yc * lax.rsqrt(var + 1e-5) * lg_ref[...] + lb_ref[...]
    o_ref[...] = (z * jax.nn.sigmoid(z)).astype(o_ref.dtype)


def _conformer_conv(p, b, t, col_a, col_g, cw, dw_w, dw_b, ln_g, ln_b):
    kw = dw_w.shape[0]
    tm = _pick(t, (256, 128))
    nt = t // tm
    hb = tm // HALO
    nrow_h = (b * t) // HALO
    ja, jg = col_a // cw, col_g // cw
    cur = lambda jc: pl.BlockSpec((tm, cw), lambda bi, i: (bi * nt + i, jc))
    prev = lambda jc: pl.BlockSpec(
        (HALO, cw), lambda bi, i: (jnp.maximum((bi * nt + i) * hb - 1, 0), jc))
    nxt = lambda jc: pl.BlockSpec(
        (HALO, cw), lambda bi, i: (jnp.minimum((bi * nt + i + 1) * hb, nrow_h - 1), jc))
    vec = lambda n: pl.BlockSpec((1, n), lambda bi, i: (0, 0))
    return pl.pallas_call(
        functools.partial(_conv_kernel, tm=tm, kw=kw, rc=64),
        grid=(b, nt),
        in_specs=[cur(ja), cur(jg), prev(ja), prev(jg), nxt(ja), nxt(jg),
                  pl.BlockSpec((kw, cw), lambda bi, i: (0, 0)), vec(cw), vec(cw), vec(cw)],
        out_specs=pl.BlockSpec((tm, cw), lambda bi, i: (bi * nt + i, 0)),
        out_shape=jax.ShapeDtypeStruct((b * t, cw), BF16),
        scratch_shapes=[pltpu.VMEM((tm + 2 * HALO, cw), F32), pltpu.VMEM((tm, cw), F32)],
        compiler_params=_params("arbitrary", "arbitrary"),
        name="conformer_conv",
    )(p, p, p, p, p, p, dw_w, dw_b.reshape(1, cw), ln_g.reshape(1, cw), ln_b.reshape(1, cw))


def _out_proj_kernel(att_ref, conv_ref, wa_ref, wc_ref, x_ref, gate_ref, o_ref):
    mix = (jnp.dot(att_ref[...], wa_ref[...], preferred_element_type=F32)
           + jnp.dot(conv_ref[...], wc_ref[...], preferred_element_type=F32))
    o_ref[...] = x_ref[...] + gate_ref[0] * mix


def _out_proj(att, conv, w_out, x2d, gate, b, t):
    m, ka = att.shape
    d = w_out.shape[1]
    tm = _pick(t, (1024, 512, 256, 128))
    tn = _pick(d, (1024, 512, 256, 128))
    nt = t // tm
    return pl.pallas_call(
        _out_proj_kernel,
        grid=(m // tm, d // tn),
        in_specs=[pl.BlockSpec((tm, ka), lambda i, j: (i, 0)),
                  pl.BlockSpec((tm, ka), lambda i, j: (i, 0)),
                  pl.BlockSpec((ka, tn), lambda i, j: (0, j)),
                  pl.BlockSpec((ka, tn), lambda i, j: (1, j)),
                  pl.BlockSpec((tm, tn), lambda i, j: (i, j)),
                  pl.BlockSpec((1, 1, tn), lambda i, j: (i // nt, 0, j))],
        out_specs=pl.BlockSpec((tm, tn), lambda i, j: (i, j)),
        out_shape=jax.ShapeDtypeStruct((m, d), F32),
        compiler_params=_params("arbitrary", "arbitrary"),
        name="out_proj",
    )(att, conv, w_out, w_out, x2d, gate)


def _ffn_up_kernel(h_ref, hp_ref, hn_ref, *refs, tm, nt, nchunk):
    o_ref, a_ref = refs[6 * nchunk:]
    i = pl.program_id(0)

    @pl.when(pl.program_id(1) == 0)
    def _():
        first = (i % nt) == 0
        last = (i % nt) == nt - 1
        a_ref[0:HALO] = jnp.where(first, jnp.zeros_like(hp_ref), hp_ref[...])
        a_ref[HALO:HALO + tm] = h_ref[...]
        a_ref[HALO + tm:] = jnp.where(last, jnp.zeros_like(hn_ref), hn_ref[...])

    ext = tm + 2 * HALO

    def conv3(w_ref, cw_ref, cb_ref):
        up = jnp.dot(a_ref[...], w_ref[...], preferred_element_type=F32)
        dn = pltpu.roll(up, 1, 0)[HALO:HALO + tm]
        nx = pltpu.roll(up, ext - 1, 0)[HALO:HALO + tm]
        return (dn * cw_ref[0:1] + up[HALO:HALO + tm] * cw_ref[1:2] + nx * cw_ref[2:3]
                + cb_ref[...])

    for c in range(nchunk):
        wu_ref, wg_ref, cwu_ref, cwg_ref, cbu_ref, cbg_ref = refs[6 * c:6 * c + 6]
        tc = wu_ref.shape[1]
        u = conv3(wu_ref, cwu_ref, cbu_ref)
        g = conv3(wg_ref, cwg_ref, cbg_ref)
        o_ref[:, c * tc:(c + 1) * tc] = (g * jax.nn.sigmoid(g) * u).astype(o_ref.dtype)


def _ffn_up(h2, w_up, dw_w, dw_b, b, t):
    m, d = h2.shape
    f = w_up.shape[1] // 2
    tm = _pick(t, (1024, 512, 256, 128))
    tc = _pick(f, (256, 128))
    nchunk = 2
    tn = nchunk * tc
    nt = t // tm
    hb = tm // HALO
    nrow_h = m // HALO
    nfc = f // tc
    kw = dw_w.shape[0]
    db = dw_b.reshape(1, -1)
    col = lambda half, c: (lambda i, j: (0, jnp.minimum(half * nfc + j * nchunk + c, 2 * nfc - 1)))
    specs, args = [], []
    for c in range(nchunk):
        for rows, arr in ((d, w_up), (kw, dw_w), (1, db)):
            for half in (0, 1):
                specs.append(pl.BlockSpec((rows, tc), col(half, c)))
                args.append(arr)
    return pl.pallas_call(
        functools.partial(_ffn_up_kernel, tm=tm, nt=nt, nchunk=nchunk),
        grid=(m // tm, pl.cdiv(f, tn)),
        in_specs=[pl.BlockSpec((tm, d), lambda i, j: (i, 0)),
                  pl.BlockSpec((HALO, d), lambda i, j: (jnp.maximum(i * hb - 1, 0), 0)),
                  pl.BlockSpec((HALO, d), lambda i, j: (jnp.minimum((i + 1) * hb, nrow_h - 1), 0))]
                 + specs,
        out_specs=pl.BlockSpec((tm, tn), lambda i, j: (i, j)),
        out_shape=jax.ShapeDtypeStruct((m, f), BF16),
        scratch_shapes=[pltpu.VMEM((tm + 2 * HALO, d), BF16)],
        compiler_params=_params("arbitrary", "arbitrary"),
        name="ffn_up",
    )(h2, h2, h2, *args)


def _ffn_down_kernel(a_ref, w_ref, x_ref, gate_ref, o_ref):
    y = jnp.dot(a_ref[...], w_ref[...], preferred_element_type=F32)
    o_ref[...] = x_ref[...] + gate_ref[0] * y


def _ffn_down(act, w_down, x1, gate, t):
    m, f = act.shape
    d = w_down.shape[1]
    tm = _pick(t, (512, 256, 128))
    tn = _pick(d, (512, 256, 128))
    nt = t // tm
    return pl.pallas_call(
        _ffn_down_kernel,
        grid=(m // tm, d // tn),
        in_specs=[pl.BlockSpec((tm, f), lambda i, j: (i, 0)),
                  pl.BlockSpec((f, tn), lambda i, j: (0, j)),
                  pl.BlockSpec((tm, tn), lambda i, j: (i, j)),
                  pl.BlockSpec((1, 1, tn), lambda i, j: (i // nt, 0, j))],
        out_specs=pl.BlockSpec((tm, tn), lambda i, j: (i, j)),
        out_shape=jax.ShapeDtypeStruct((m, d), F32),
        compiler_params=_params("arbitrary", "arbitrary"),
        name="ffn_down",
    )(act, w_down, x1, gate)


def _layer(x, ctx, mods, lam_init, norm1_g, norm2_g, w_in, q_norm_g, k_norm_g, lq1, lk1, lq2, lk2,
           subln_g, conv_dw_w, conv_dw_b, conv_ln_g, conv_ln_b, w_out, w_up, ffn_dw_w, ffn_dw_b,
           w_down):
    b, t, d = x.shape
    n_ctx = ctx.shape[1]
    mix_w = w_out.shape[0]
    att_w = mix_w // 2
    conv_w = mix_w - att_w
    n_heads = att_w // V_HEAD_DIM
    qk_w = n_heads * 2 * HEAD_DIM
    v_w = n_heads * V_HEAD_DIM

    mod_x = mods[:b].reshape(b, N_MOD, 1, d)
    mod_c = jnp.broadcast_to(mods[b].reshape(1, N_MOD, 1, d), (b, N_MOD, 1, d))

    w_in_b = w_in.astype(BF16)
    h_x = _norm_mod(x, norm1_g, mod_x[:, 0], mod_x[:, 1]).reshape(b * t, d)
    h_c = _norm_mod(ctx, norm1_g, mod_c[:, 0], mod_c[:, 1]).reshape(b * n_ctx, d)

    p_x = _matmul(h_x, w_in_b, F32)
    p_c = _matmul(h_c, w_in_b, F32, col0=qk_w, ncols=qk_w + v_w)

    tk = _kv_block(t, n_ctx)
    qt, k_x, vt_x = _qkv_post(p_x, b, t, n_heads, q_norm_g, k_norm_g, tk)
    k_c, vt_c = _kv_post(p_c, b, n_ctx, n_heads, k_norm_g, tk)
    k = jnp.concatenate([k_c, k_x], axis=2)
    vt = jnp.concatenate([vt_c, vt_x], axis=2)
    att = _attention(qt, k, vt, lq1, lk1, lq2, lk2, subln_g, lam_init).reshape(b * t, v_w)

    conv = _conformer_conv(p_x, b, t, 2 * qk_w + v_w, 2 * qk_w + v_w + conv_w, conv_w,
                           conv_dw_w, conv_dw_b, conv_ln_g, conv_ln_b)

    x1 = _out_proj(att, conv, w_out.astype(BF16), x.reshape(b * t, d), mod_x[:, 2], b, t)
    h2 = _norm_mod(x1.reshape(b, t, d), norm2_g, mod_x[:, 3], mod_x[:, 4]).reshape(b * t, d)
    act = _ffn_up(h2, w_up.astype(BF16), ffn_dw_w, ffn_dw_b, b, t)
    out = _ffn_down(act, w_down.astype(BF16), x1, mod_x[:, 5], t)
    return out.reshape(b, t, d)


def kernel(x, c, ctx, c_ctx, w_ada, b_ada, norm1_g, norm2_g, w_in, q_norm_g, k_norm_g, lambda_q1,
           lambda_k1, lambda_q2, lambda_k2, subln_g, conv_dw_w, conv_dw_b, conv_ln_g, conv_ln_b,
           w_out, w_up, ffn_dw_w, ffn_dw_b, w_down):
    b, t, d = x.shape
    depth = w_ada.shape[0]
    assert depth == 1, "context-stream updates are only needed for depth > 1"
    i = 0
    lam_init = 0.8 - 0.6 * math.exp(-0.3 * i)
    cc = jnp.zeros((8, d), F32).at[:b].set(c).at[b].set(c_ctx)
    mods = _ada(cc, w_ada[i], b_ada[i])
    return _layer(x, ctx, mods, lam_init, norm1_g[i], norm2_g[i], w_in[i], q_norm_g[i],
                  k_norm_g[i], lambda_q1[i], lambda_k1[i], lambda_q2[i], lambda_k2[i], subln_g[i],
                  conv_dw_w[i], conv_dw_b[i], conv_ln_g[i], conv_ln_b[i], w_out[i], w_up[i],
                  ffn_dw_w[i], ffn_dw_b[i], w_down[i])
```
